```python
import jax, jax.numpy as jnp
from jax import lax
import numpy as np

D_MODEL = 1024
BATCH = 4
SEQ = 4096
DEPTH = 4

N_MIXERS = 2
N_MLA_LAYERS = (DEPTH + 1) // 2
N_SGU_LAYERS = DEPTH // 2
MLA_HEADS = 8
QK_NOPE_DIM = 128
QK_ROPE_DIM = 64
QK_HEAD_DIM = QK_NOPE_DIM + QK_ROPE_DIM
V_HEAD_DIM = 128
Q_LORA_RANK = 256
KV_LORA_RANK = 128
ROPE_THETA = 10000.0
Q_BLOCK = 128
SGU_CHUNK = 128
SGU_WIDTH = 2 * D_MODEL
SGU_GROUPS = 8
SGU_GROUP_DIM = SGU_WIDTH // SGU_GROUPS
FFN_HIDDEN = 4 * D_MODEL
NORM_EPS = 1e-6
LN_EPS = 1e-5

kernel_name = "hybrid_mla_chunked_sgu_trunk"


def rms_norm(x, g):
    xf = x.astype(jnp.float32)
    y = xf * lax.rsqrt(jnp.mean(xf * xf, axis=-1, keepdims=True) + NORM_EPS)
    return (y * g.astype(jnp.float32)).astype(x.dtype)


def layer_norm(x, g, b):
    xf = x.astype(jnp.float32)
    mu = jnp.mean(xf, axis=-1, keepdims=True)
    var = jnp.mean(jnp.square(xf - mu), axis=-1, keepdims=True)
    y = (xf - mu) * lax.rsqrt(var + LN_EPS)
    return (y * g.astype(jnp.float32) + b.astype(jnp.float32)).astype(x.dtype)


def apply_rope(x, cos, sin):
    x1, x2 = jnp.split(x.astype(jnp.float32), 2, axis=-1)
    out = jnp.concatenate([x1 * cos - x2 * sin, x2 * cos + x1 * sin], axis=-1)
    return out.astype(x.dtype)


def mla_mixer(h, positions, w_dkv, q_norm, kv_norm, w_uq, w_ukv, w_o):
    B, S, _ = h.shape
    lat = h @ w_dkv
    c_q, c_kv, k_rope = jnp.split(lat, [Q_LORA_RANK, Q_LORA_RANK + KV_LORA_RANK], axis=-1)
    c_q = rms_norm(c_q, q_norm)
    c_kv = rms_norm(c_kv, kv_norm)
    q = (c_q @ w_uq).reshape(B, S, MLA_HEADS, QK_HEAD_DIM)
    q_nope, q_rope = jnp.split(q, [QK_NOPE_DIM], axis=-1)
    kv = (c_kv @ w_ukv).reshape(B, S, MLA_HEADS, QK_NOPE_DIM + V_HEAD_DIM)
    k_nope, v = jnp.split(kv, [QK_NOPE_DIM], axis=-1)

    inv_freq = ROPE_THETA ** (-jnp.arange(0, QK_ROPE_DIM, 2, dtype=jnp.float32) / QK_ROPE_DIM)
    ang = positions.astype(jnp.float32)[..., None] * inv_freq
    cos, sin = jnp.cos(ang), jnp.sin(ang)
    q_rope = apply_rope(q_rope, cos[:, :, None, :], sin[:, :, None, :])
    k_rope = apply_rope(k_rope, cos, sin)

    nb = S // Q_BLOCK
    qn_b = q_nope.reshape(B, nb, Q_BLOCK, MLA_HEADS, QK_NOPE_DIM).transpose(1, 0, 2, 3, 4)
    qr_b = q_rope.reshape(B, nb, Q_BLOCK, MLA_HEADS, QK_ROPE_DIM).transpose(1, 0, 2, 3, 4)
    key_idx = jnp.arange(S)
    scale = QK_HEAD_DIM ** -0.5

    def attend(args):
        qn, qr, blk = args
        s = (jnp.einsum('bqhd,bkhd->bhqk', qn, k_nope)
             + jnp.einsum('bqhr,bkr->bhqk', qr, k_rope))
        s = s.astype(jnp.float32) * scale
        q_idx = blk * Q_BLOCK + jnp.arange(Q_BLOCK)
        causal = key_idx[None, :] <= q_idx[:, None]
        s = jnp.where(causal[None, None], s, -jnp.inf)
        p = jax.nn.softmax(s, axis=-1).astype(v.dtype)
        return jnp.einsum('bhqk,bkhd->bqhd', p, v)

    o = lax.map(attend, (qn_b, qr_b, jnp.arange(nb)))
    o = o.transpose(1, 0, 2, 3, 4).reshape(B, S, MLA_HEADS * V_HEAD_DIM)
    return o @ w_o


def chunked_sgu_mixer(h, w_in, ln_g, ln_b, w_spatial, b_spatial, w_out):
    B, S, _ = h.shape
    z = jax.nn.gelu(h @ w_in, approximate=False)
    u, v = jnp.split(z, 2, axis=-1)
    v = layer_norm(v, ln_g, ln_b)
    nc = S // SGU_CHUNK
    vg = v.reshape(B, nc, SGU_CHUNK, SGU_GROUPS, SGU_GROUP_DIM)
    w_causal = jnp.tril(w_spatial)
    mixed = (jnp.einsum('gts,bcsgd->bctgd', w_causal, vg)
             + b_spatial.T[None, None, :, :, None])
    v = mixed.reshape(B, S, SGU_WIDTH)
    return (u * v) @ w_out


def sq_relu_mlp(h, w_up, w_down):
    return jnp.square(jax.nn.relu(h @ w_up)) @ w_down


def setup_inputs(seed: int = 0) -> dict:
    key = jax.random.key(seed)
    ks = jax.random.split(key, 24)
    f32 = jnp.float32

    def nrm(k, shape, fan_in, gain=1.0):
        return jax.random.normal(k, shape, f32) * (gain * fan_in ** -0.5)

    def gain(k, shape):
        return 1.0 + 0.02 * jax.random.normal(k, shape, f32)

    x = jax.random.normal(ks[0], (BATCH, SEQ, D_MODEL), f32)
    offset = jax.random.randint(ks[1], (BATCH, 1), 0, 1024, dtype=jnp.int32)
    positions = offset + jnp.arange(SEQ, dtype=jnp.int32)[None, :]

    return {
        "x": x,
        "positions": positions,
        "norm_mix": gain(ks[2], (DEPTH, D_MODEL)),
        "norm_ffn": gain(ks[3], (DEPTH, D_MODEL)),
        "final_norm": gain(ks[4], (D_MODEL,)),
        "mla_w_dkv": nrm(ks[5], (N_MLA_LAYERS, D_MODEL, Q_LORA_RANK + KV_LORA_RANK + QK_ROPE_DIM), D_MODEL),
        "mla_q_norm": gain(ks[6], (N_MLA_LAYERS, Q_LORA_RANK)),
        "mla_kv_norm": gain(ks[7], (N_MLA_LAYERS, KV_LORA_RANK)),
        "mla_w_uq": nrm(ks[8], (N_MLA_LAYERS, Q_LORA_RANK, MLA_HEADS * QK_HEAD_DIM), Q_LORA_RANK),
        "mla_w_ukv": nrm(ks[9], (N_MLA_LAYERS, KV_LORA_RANK, MLA_HEADS * (QK_NOPE_DIM + V_HEAD_DIM)), KV_LORA_RANK),
        "mla_w_o": nrm(ks[10], (N_MLA_LAYERS, MLA_HEADS * V_HEAD_DIM, D_MODEL), MLA_HEADS * V_HEAD_DIM),
        "sgu_w_in": nrm(ks[11], (N_SGU_LAYERS, D_MODEL, 2 * SGU_WIDTH), D_MODEL),
        "sgu_ln_g": gain(ks[12], (N_SGU_LAYERS, SGU_WIDTH)),
        "sgu_ln_b": 0.02 * jax.random.normal(ks[13], (N_SGU_LAYERS, SGU_WIDTH), f32),
        "sgu_w_spatial": nrm(ks[14], (N_SGU_LAYERS, SGU_GROUPS, SGU_CHUNK, SGU_CHUNK), SGU_CHUNK, 0.5),
        "sgu_b_spatial": gain(ks[15], (N_SGU_LAYERS, SGU_GROUPS, SGU_CHUNK)),
        "sgu_w_out": nrm(ks[16], (N_SGU_LAYERS, SGU_WIDTH, D_MODEL), SGU_WIDTH),
        "ffn_w_up": nrm(ks[17], (DEPTH, D_MODEL, FFN_HIDDEN), D_MODEL),
        "ffn_w_down": nrm(ks[18], (DEPTH, FFN_HIDDEN, D_MODEL), FFN_HIDDEN),
    }


def reference(x, positions, norm_mix, norm_ffn, final_norm,
              mla_w_dkv, mla_q_norm, mla_kv_norm, mla_w_uq, mla_w_ukv, mla_w_o,
              sgu_w_in, sgu_ln_g, sgu_ln_b, sgu_w_spatial, sgu_b_spatial, sgu_w_out,
              ffn_w_up, ffn_w_down):
    for i in range(DEPTH):
        h = rms_norm(x, norm_mix[i])
        j = i // N_MIXERS
        if i % N_MIXERS == 0:
            x = x + mla_mixer(h, positions, mla_w_dkv[j], mla_q_norm[j], mla_kv_norm[j],
                              mla_w_uq[j], mla_w_ukv[j], mla_w_o[j])
        else:
            x = x + chunked_sgu_mixer(h, sgu_w_in[j], sgu_ln_g[j], sgu_ln_b[j],
                                      sgu_w_spatial[j], sgu_b_spatial[j], sgu_w_out[j])
        h = rms_norm(x, norm_ffn[i])
        x = x + sq_relu_mlp(h, ffn_w_up[i], ffn_w_down[i])
    return rms_norm(x, final_norm)
```

```python
import functools

import jax
import jax.numpy as jnp
from jax import lax
from jax.experimental import pallas as pl
from jax.experimental.pallas import tpu as pltpu

F32 = jnp.float32
BF16 = jnp.bfloat16

MLA_HEADS = 8
QK_NOPE_DIM = 128
QK_ROPE_DIM = 64
QK_HEAD_DIM = QK_NOPE_DIM + QK_ROPE_DIM
V_HEAD_DIM = 128
Q_LORA_RANK = 256
KV_LORA_RANK = 128
ROPE_THETA = 10000.0
SGU_CHUNK = 128
SGU_GROUPS = 8
NORM_EPS = 1e-6
LN_EPS = 1e-5

V7X_LANES = 128
V7X_VMEM_BYTES = 64 * 1024 * 1024
V7X_VMEM_RESERVE_BYTES = 8 * 1024 * 1024

QK_PAD = KV_LORA_RANK + V7X_LANES

ROPE_TABLE_ROWS = 2048
PROJ_ROWS = 256
ATTN_Q_ROWS = 256
ATTN_K_ROWS = 256
MLP_ROWS = 512


def _vmem_limit(block_bytes, temp_bytes):
  want = block_bytes + temp_bytes + V7X_VMEM_RESERVE_BYTES
  return int(min(want, V7X_VMEM_BYTES - V7X_VMEM_RESERVE_BYTES))


def _nbytes(shape, dtype):
  n = 1
  for s in shape:
    n *= s
  return n * jnp.dtype(dtype).itemsize


def _dot(a, b):
  return jnp.dot(a, b, preferred_element_type=F32)


def _rms(x, g):
  return x * lax.rsqrt(jnp.mean(x * x, axis=-1, keepdims=True) + NORM_EPS) * g


def _resident(shape):
  nd = len(shape)
  return pl.BlockSpec(shape, lambda *_: (0,) * nd, pipeline_mode=pl.Buffered(1))


def _rope_table_kernel(pos_ref, freq_ref, cos_ref, sin_ref):
  ang = pos_ref[...] * freq_ref[...]
  cos_ref[...] = jnp.cos(ang)
  sin_ref[...] = jnp.sin(ang)


def _rope_tables(pos_col, freq_row):
  t = pos_col.shape[0]
  rows = ROPE_TABLE_ROWS
  out = jax.ShapeDtypeStruct((t, V7X_LANES), F32)
  return pl.pallas_call(
      _rope_table_kernel,
      grid=(t // rows,),
      in_specs=[pl.BlockSpec((rows, 1), lambda i: (i, 0)),
                pl.BlockSpec((1, V7X_LANES), lambda i: (0, 0))],
      out_specs=[pl.BlockSpec((rows, V7X_LANES), lambda i: (i, 0))] * 2,
      out_shape=[out, out],
      name="rope_tables",
  )(pos_col, freq_row)


def _rope(x, cos, sin_signed):
  width = x.shape[1]
  reps = width // V7X_LANES
  lane = lax.broadcasted_iota(jnp.int32, x.shape, 1)
  first_half = (lane % QK_ROPE_DIM) < (QK_ROPE_DIM // 2)
  swapped = jnp.where(first_half,
                      pltpu.roll(x, width - QK_ROPE_DIM // 2, 1),
                      pltpu.roll(x, QK_ROPE_DIM // 2, 1))
  if reps > 1:
    cos = jnp.concatenate([cos] * reps, axis=1)
    sin_signed = jnp.concatenate([sin_signed] * reps, axis=1)
  return x * cos + swapped * sin_signed


def _mla_proj_kernel(x_ref, g_ref, wdkv_ref, qn_ref, kvn_ref, wqn_ref, wqr_ref,
                     wukt_ref, cos_ref, sin_ref, q_ref, kc_ref, ct_ref):
  h = _rms(x_ref[...], g_ref[...]).astype(BF16)
  lat = _dot(h, wdkv_ref[...])
  c_q = _rms(lat[:, :Q_LORA_RANK], qn_ref[...]).astype(BF16)
  c_kv = _rms(lat[:, Q_LORA_RANK:Q_LORA_RANK + KV_LORA_RANK], kvn_ref[...])
  k_rope2 = lat[:, Q_LORA_RANK + KV_LORA_RANK:]

  cos = cos_ref[...]
  lane = lax.broadcasted_iota(jnp.int32, cos.shape, 1)
  sin_signed = jnp.where((lane % QK_ROPE_DIM) < (QK_ROPE_DIM // 2),
                         -sin_ref[...], sin_ref[...])

  c_kv_bf = c_kv.astype(BF16)
  kc_ref[...] = jnp.concatenate(
      [c_kv_bf, _rope(k_rope2, cos, sin_signed).astype(BF16)], axis=1)
  ct_ref[0] = c_kv.T.astype(BF16)

  scale = QK_HEAD_DIM ** -0.5
  q_nope = _dot(c_q, wqn_ref[...])
  q_rope = _rope(_dot(c_q, wqr_ref[...]), cos, sin_signed) * scale
  low_half = lane < QK_ROPE_DIM
  for hd in range(MLA_HEADS):
    qn_h = q_nope[:, hd * QK_NOPE_DIM:(hd + 1) * QK_NOPE_DIM].astype(BF16)
    q_lat = _dot(qn_h, wukt_ref[hd]) * scale
    pair = q_rope[:, (hd // 2) * V7X_LANES:(hd // 2 + 1) * V7X_LANES]
    keep = low_half if hd % 2 == 0 else jnp.logical_not(low_half)
    q_ref[0, hd] = jnp.concatenate(
        [q_lat.astype(BF16), jnp.where(keep, pair, 0.0).astype(BF16)], axis=1)


def _mla_proj(x, g, wdkv, qn, kvn, wqn, wqr, wukt, cos, sin, batch, seq):
  t, d = x.shape
  rows = PROJ_ROWS
  steps_per_batch = seq // rows
  k_tiles = seq // ATTN_K_ROWS
  assert rows == ATTN_K_ROWS
  row_spec = lambda width: pl.BlockSpec((rows, width), lambda i: (i, 0))
  in_specs = [row_spec(d), _resident(g.shape), _resident(wdkv.shape), _resident(qn.shape),
              _resident(kvn.shape), _resident(wqn.shape), _resident(wqr.shape),
              _resident(wukt.shape), row_spec(V7X_LANES), row_spec(V7X_LANES)]
  out_shape = [
      jax.ShapeDtypeStruct((batch, MLA_HEADS, seq, QK_PAD), BF16),
      jax.ShapeDtypeStruct((t, QK_PAD), BF16),
      jax.ShapeDtypeStruct((batch * k_tiles, KV_LORA_RANK, rows), BF16),
  ]
  out_specs = [
      pl.BlockSpec((1, MLA_HEADS, rows, QK_PAD),
                   lambda i: (i // steps_per_batch, 0, i % steps_per_batch, 0)),
      row_spec(QK_PAD),
      pl.BlockSpec((1, KV_LORA_RANK, rows), lambda i: (i, 0, 0)),
  ]
  weights = sum(_nbytes(w.shape, w.dtype) for w in (wdkv, wqn, wqr, wukt))
  blocks = weights + 2 * (_nbytes((rows, d), F32) + 2 * _nbytes((rows, V7X_LANES), F32)
                          + _nbytes((MLA_HEADS + 2, rows, QK_PAD), BF16))
  temps = 6 * _nbytes((rows, MLA_HEADS * QK_HEAD_DIM), F32)
  return pl.pallas_call(
      _mla_proj_kernel,
      grid=(t // rows,),
      in_specs=in_specs,
      out_specs=out_specs,
      out_shape=out_shape,
      compiler_params=pltpu.CompilerParams(
          dimension_semantics=("arbitrary",),
          vmem_limit_bytes=_vmem_limit(blocks, temps)),
      name="mla_proj",
  )(x, g, wdkv, qn, kvn, wqn, wqr, wukt, cos, sin)


def _mla_attn_kernel(q_ref, kc_ref, ct_ref, o_ref, m_ref, l_ref, acc_ref):
  qt = pl.program_id(1)
  tq = q_ref.shape[2]
  tk = ct_ref.shape[3]
  nq = MLA_HEADS * tq
  q = q_ref[0].reshape(nq, QK_PAD)

  m_ref[...] = jnp.full(m_ref.shape, -jnp.inf, F32)
  l_ref[...] = jnp.zeros(l_ref.shape, F32)
  acc_ref[...] = jnp.zeros(acc_ref.shape, F32)

  def block(kt, masked):
    k = kc_ref[0, pl.ds(pl.multiple_of(kt * tk, tk), tk), :]
    s = lax.dot_general(k, q, (((1,), (1,)), ((), ())),
                        preferred_element_type=F32)
    if masked:
      key = lax.broadcasted_iota(jnp.int32, s.shape, 0)
      qry = lax.broadcasted_iota(jnp.int32, s.shape, 1) % tq
      s = jnp.where(key <= qry, s, -jnp.inf)
    m_prev = m_ref[...]
    m_new = jnp.maximum(m_prev, jnp.max(s, axis=0, keepdims=True))
    alpha = jnp.exp(m_prev - m_new)
    p = jnp.exp(s - m_new)
    l_ref[...] = alpha * l_ref[...] + jnp.sum(p, axis=0, keepdims=True)
    acc_ref[...] = alpha * acc_ref[...] + _dot(ct_ref[0, kt], p.astype(BF16))
    m_ref[...] = m_new

  def body(kt, carry):
    block(kt, False)
    return carry

  lax.fori_loop(0, qt, body, 0)
  block(qt, True)

  o_t = acc_ref[...] / l_ref[...]
  for hd in range(MLA_HEADS):
    o_ref[0, :, hd * KV_LORA_RANK:(hd + 1) * KV_LORA_RANK] = (
        o_t[:, hd * tq:(hd + 1) * tq].T.astype(o_ref.dtype))


def _mla_attn(q, kc, ct, batch, seq):
  tq, tk = ATTN_Q_ROWS, ATTN_K_ROWS
  assert tq == tk
  nq = MLA_HEADS * tq
  k_tiles = seq // tk
  kc3 = kc.reshape(batch, seq, QK_PAD)
  ct4 = ct.reshape(batch, k_tiles, KV_LORA_RANK, tk)
  out_w = MLA_HEADS * KV_LORA_RANK
  blocks = 2 * (_nbytes((nq, QK_PAD), BF16) + _nbytes((seq, QK_PAD), BF16)
                + _nbytes((KV_LORA_RANK, seq), BF16) + _nbytes((tq, out_w), BF16))
  scratch = _nbytes((KV_LORA_RANK + 16, nq), F32)
  temps = 3 * _nbytes((tk, nq), F32)
  return pl.pallas_call(
      _mla_attn_kernel,
      grid=(batch, seq // tq),
      in_specs=[
          pl.BlockSpec((1, MLA_HEADS, tq, QK_PAD), lambda b, i: (b, 0, i, 0)),
          pl.BlockSpec((1, seq, QK_PAD), lambda b, i: (b, 0, 0)),
          pl.BlockSpec((1, k_tiles, KV_LORA_RANK, tk), lambda b, i: (b, 0, 0, 0)),
      ],
      out_specs=pl.BlockSpec((1, tq, out_w), lambda b, i: (b, i, 0)),
      out_shape=jax.ShapeDtypeStruct((batch, seq, out_w), BF16),
      scratch_shapes=[pltpu.VMEM((1, nq), F32), pltpu.VMEM((1, nq), F32),
                      pltpu.VMEM((KV_LORA_RANK, nq), F32)],
      compiler_params=pltpu.CompilerParams(
          dimension_semantics=("arbitrary", "arbitrary"),
          vmem_limit_bytes=_vmem_limit(blocks + scratch, temps)),
      name="mla_attn",
  )(q, kc3, ct4)


def _mla_out_kernel(x_ref, ol_ref, wuv_ref, wo_ref, o_ref):
  ol = ol_ref[...]
  heads = []
  for hd in range(MLA_HEADS):
    heads.append(_dot(ol[:, hd * KV_LORA_RANK:(hd + 1) * KV_LORA_RANK],
                      wuv_ref[hd]).astype(BF16))
  o = jnp.concatenate(heads, axis=1)
  o_ref[...] = x_ref[...] + _dot(o, wo_ref[...])


def _mla_out(x, o_lat, wuv, wo):
  t, d = x.shape
  rows = MLP_ROWS
  blocks = (_nbytes(wuv.shape, BF16) + _nbytes(wo.shape, BF16)
            + 2 * (2 * _nbytes((rows, d), F32) + _nbytes((rows, o_lat.shape[1]), BF16)))
  temps = 3 * _nbytes((rows, d), F32)
  return pl.pallas_call(
      _mla_out_kernel,
      grid=(t // rows,),
      in_specs=[pl.BlockSpec((rows, d), lambda i: (i, 0)),
                pl.BlockSpec((rows, o_lat.shape[1]), lambda i: (i, 0)),
                _resident(wuv.shape), _resident(wo.shape)],
      out_specs=pl.BlockSpec((rows, d), lambda i: (i, 0)),
      out_shape=jax.ShapeDtypeStruct((t, d), F32),
      compiler_params=pltpu.CompilerParams(
          dimension_semantics=("arbitrary",),
          vmem_limit_bytes=_vmem_limit(blocks, temps)),
      name="mla_out",
  )(x, o_lat, wuv, wo)


def _sgu_kernel(x_ref, g_ref, win_ref, lng_ref, lnb_ref, wsp_ref, bsp_ref, wout_ref,
                o_ref, gated_ref):
  x = x_ref[...]
  rows = x.shape[0]
  width = lng_ref.shape[1]
  group_dim = width // SGU_GROUPS
  h = _rms(x, g_ref[...]).astype(BF16)
  z = _dot(h, win_ref[...])
  z = 0.5 * z * (1.0 + lax.erf(z * (0.5 ** 0.5)))
  u = z[:, :width]
  v = z[:, width:]
  mu = jnp.mean(v, axis=-1, keepdims=True)
  var = jnp.mean(jnp.square(v - mu), axis=-1, keepdims=True)
  v = ((v - mu) * lax.rsqrt(var + LN_EPS) * lng_ref[...] + lnb_ref[...]).astype(BF16)

  t_idx = lax.broadcasted_iota(jnp.int32, (SGU_CHUNK, SGU_CHUNK), 0)
  s_idx = lax.broadcasted_iota(jnp.int32, (SGU_CHUNK, SGU_CHUNK), 1)
  causal = s_idx <= t_idx
  for gi in range(SGU_GROUPS):
    w_g = jnp.where(causal, wsp_ref[gi], 0.0).astype(BF16)
    bias = jnp.broadcast_to(bsp_ref[:, gi:gi + 1], (SGU_CHUNK, group_dim))
    cols = slice(gi * group_dim, (gi + 1) * group_dim)
    for ci in range(rows // SGU_CHUNK):
      rws = slice(ci * SGU_CHUNK, (ci + 1) * SGU_CHUNK)
      mixed = _dot(w_g, v[rws, cols]) + bias
      gated_ref[rws, cols] = (u[rws, cols] * mixed).astype(BF16)
  o_ref[...] = x + _dot(gated_ref[...], wout_ref[...])


def _sgu(x, g, win, lng, lnb, wsp, bsp_t, wout):
  t, d = x.shape
  rows = MLP_ROWS
  width = lng.shape[1]
  blocks = (_nbytes(win.shape, BF16) + _nbytes(wout.shape, BF16) + _nbytes(wsp.shape, F32)
            + 4 * _nbytes((rows, d), F32) + _nbytes((rows, width), BF16))
  temps = 3 * _nbytes((rows, 2 * width), F32)
  return pl.pallas_call(
      _sgu_kernel,
      grid=(t // rows,),
      in_specs=[pl.BlockSpec((rows, d), lambda i: (i, 0)), _resident(g.shape),
                _resident(win.shape), _resident(lng.shape), _resident(lnb.shape),
                _resident(wsp.shape), _resident(bsp_t.shape), _resident(wout.shape)],
      out_specs=pl.BlockSpec((rows, d), lambda i: (i, 0)),
      out_shape=jax.ShapeDtypeStruct((t, d), F32),
      scratch_shapes=[pltpu.VMEM((rows, width), BF16)],
      compiler_params=pltpu.CompilerParams(
          dimension_semantics=("arbitrary",),
          vmem_limit_bytes=_vmem_limit(blocks, temps)),
      name="sgu_mixer",
  )(x, g, win, lng, lnb, wsp, bsp_t, wout)


def _ffn_kernel(x_ref, g_ref, wup_ref, wdn_ref, *rest, final_norm):
  o_ref = rest[-1]
  x = x_ref[...]
  h = _rms(x, g_ref[...]).astype(BF16)
  a = jnp.square(jnp.maximum(_dot(h, wup_ref[...]), 0.0)).astype(BF16)
  y = x + _dot(a, wdn_ref[...])
  if final_norm:
    y = _rms(y, rest[0][...])
  o_ref[...] = y


def _ffn(x, g, wup, wdn, final_g=None):
  t, d = x.shape
  rows = MLP_ROWS
  hidden = wup.shape[1]
  args = [x, g, wup, wdn]
  in_specs = [pl.BlockSpec((rows, d), lambda i: (i, 0)), _resident(g.shape),
              _resident(wup.shape), _resident(wdn.shape)]
  if final_g is not None:
    args.append(final_g)
    in_specs.append(_resident(final_g.shape))
  blocks = _nbytes(wup.shape, BF16) + _nbytes(wdn.shape, BF16) + 4 * _nbytes((rows, d), F32)
  temps = _nbytes((rows, hidden), F32) + _nbytes((rows, hidden), BF16) + 2 * _nbytes((rows, d), F32)
  return pl.pallas_call(
      functools.partial(_ffn_kernel, final_norm=final_g is not None),
      grid=(t // rows,),
      in_specs=in_specs,
      out_specs=pl.BlockSpec((rows, d), lambda i: (i, 0)),
      out_shape=jax.ShapeDtypeStruct((t, d), F32),
      compiler_params=pltpu.CompilerParams(
          dimension_semantics=("arbitrary",),
          vmem_limit_bytes=_vmem_limit(blocks, temps)),
      name="ffn",
  )(*args)


def kernel(x, positions, norm_mix, norm_ffn, final_norm, mla_w_dkv, mla_q_norm, mla_kv_norm, mla_w_uq, mla_w_ukv, mla_w_o, sgu_w_in, sgu_ln_g, sgu_ln_b, sgu_w_spatial, sgu_b_spatial, sgu_w_out, ffn_w_up, ffn_w_down):
  batch, seq, d = x.shape
  depth = norm_mix.shape[0]
  t = batch * seq
  assert seq % MLP_ROWS == 0 and seq % ATTN_Q_ROWS == 0 and t % ROPE_TABLE_ROWS == 0
  xf = x.reshape(t, d)

  inv_freq = ROPE_THETA ** (-jnp.arange(0, QK_ROPE_DIM, 2, dtype=F32) / QK_ROPE_DIM)
  freq_row = jnp.tile(inv_freq, V7X_LANES // inv_freq.shape[0]).reshape(1, V7X_LANES)
  cos, sin = _rope_tables(positions.astype(F32).reshape(t, 1), freq_row)

  row = lambda a: a.reshape(1, -1)
  for i in range(depth):
    j = i // 2
    if i % 2 == 0:
      wdkv = mla_w_dkv[j]
      wdkv = jnp.concatenate([wdkv, wdkv[:, Q_LORA_RANK + KV_LORA_RANK:]], axis=1).astype(BF16)
      wuq = mla_w_uq[j].reshape(Q_LORA_RANK, MLA_HEADS, QK_HEAD_DIM)
      wqn = wuq[:, :, :QK_NOPE_DIM].reshape(Q_LORA_RANK, -1).astype(BF16)
      wqr = wuq[:, :, QK_NOPE_DIM:].reshape(Q_LORA_RANK, -1).astype(BF16)
      wukv = mla_w_ukv[j].reshape(KV_LORA_RANK, MLA_HEADS, QK_NOPE_DIM + V_HEAD_DIM)
      wukt = wukv[:, :, :QK_NOPE_DIM].transpose(1, 2, 0).astype(BF16)
      wuv = wukv[:, :, QK_NOPE_DIM:].transpose(1, 0, 2).astype(BF16)
      q, kc, ct = _mla_proj(xf, row(norm_mix[i]), wdkv, row(mla_q_norm[j]),
                            row(mla_kv_norm[j]), wqn, wqr, wukt, cos, sin, batch, seq)
      o_lat = _mla_attn(q, kc, ct, batch, seq)
      xf = _mla_out(xf, o_lat.reshape(t, -1), wuv, mla_w_o[j].astype(BF16))
    else:
      xf = _sgu(xf, row(norm_mix[i]), sgu_w_in[j].astype(BF16), row(sgu_ln_g[j]),
                row(sgu_ln_b[j]), sgu_w_spatial[j], sgu_b_spatial[j].T,
                sgu_w_out[j].astype(BF16))
    last = i == depth - 1
    xf = _ffn(xf, row(norm_ffn[i]), ffn_w_up[i].astype(BF16), ffn_w_down[i].astype(BF16),
              final_g=row(final_norm) if last else None)
  return xf.reshape(batch, seq, d)
```

```python
import functools
import math

import jax
import jax.numpy as jnp
from jax import lax
from jax.experimental import pallas as pl
from jax.experimental.pallas import tpu as pltpu

F32 = jnp.float32
BF16 = jnp.bfloat16

MLA_HEADS = 8
QK_NOPE_DIM = 128
QK_ROPE_DIM = 64
QK_HEAD_DIM = QK_NOPE_DIM + QK_ROPE_DIM
V_HEAD_DIM = 128
Q_LORA_RANK = 256
KV_LORA_RANK = 128
ROPE_THETA = 10000.0
SGU_CHUNK = 128
SGU_GROUPS = 8
NORM_EPS = 1e-6
LN_EPS = 1e-5

V7X_LANES = 128
V7X_VMEM_BYTES = 64 * 1024 * 1024
V7X_VMEM_RESERVE_BYTES = 8 * 1024 * 1024

QK_PAD = KV_LORA_RANK + V7X_LANES
ROPE_HALF = QK_ROPE_DIM // 2

ROPE_TABLE_COLS = 2048
ATTN_ROWS = 256
MLP_ROWS = 512


def _vmem_limit(block_bytes, temp_bytes):
  want = block_bytes + temp_bytes + V7X_VMEM_RESERVE_BYTES
  return int(min(want, V7X_VMEM_BYTES - V7X_VMEM_RESERVE_BYTES))


def _nbytes(shape, dtype):
  return math.prod(shape) * jnp.dtype(dtype).itemsize


def _dot(a, b):
  return jnp.dot(a, b, preferred_element_type=F32)


def _rms(x, g):
  return x * lax.rsqrt(jnp.mean(x * x, axis=-1, keepdims=True) + NORM_EPS) * g


def _resident(shape):
  nd = len(shape)
  return pl.BlockSpec(shape, lambda *_: (0,) * nd, pipeline_mode=pl.Buffered(1))


def _rope_table_kernel(pos_ref, freq_ref, cos_ref, sin_ref):
  ang = freq_ref[...] * pos_ref[...]
  cos_ref[...] = jnp.cos(ang)
  sin_ref[...] = jnp.sin(ang)


def _rope_tables(pos_row, freq_col):
  t = pos_row.shape[1]
  cols = ROPE_TABLE_COLS
  out = jax.ShapeDtypeStruct((ROPE_HALF, t), F32)
  return pl.pallas_call(
      _rope_table_kernel,
      grid=(t // cols,),
      in_specs=[pl.BlockSpec((1, cols), lambda i: (0, i)),
                pl.BlockSpec((ROPE_HALF, 1), lambda i: (0, 0))],
      out_specs=[pl.BlockSpec((ROPE_HALF, cols), lambda i: (0, i))] * 2,
      out_shape=[out, out],
      name="rope_tables",
  )(pos_row, freq_col)


def _rope_t(x_t, cos_t, sin_t):
  x1, x2 = x_t[:ROPE_HALF], x_t[ROPE_HALF:]
  return jnp.concatenate([x1 * cos_t - x2 * sin_t, x2 * cos_t + x1 * sin_t], axis=0)


def _mla_proj_kernel(x_ref, g_ref, wdkv_ref, qn_ref, kvn_ref, wqnt_ref, wqrt_ref,
                     wuk_ref, cos_ref, sin_ref, qt_ref, kc_ref, ct_ref):
  rows = x_ref.shape[0]
  h = _rms(x_ref[...], g_ref[...]).astype(BF16)
  lat = _dot(h, wdkv_ref[...])
  c_q = _rms(lat[:, :Q_LORA_RANK], qn_ref[...])
  c_kv = _rms(lat[:, Q_LORA_RANK:Q_LORA_RANK + KV_LORA_RANK], kvn_ref[...])
  cos_t = cos_ref[...]
  sin_t = sin_ref[...]
  pad = jnp.zeros((V7X_LANES - QK_ROPE_DIM, rows), F32)

  kr_t = lat[:, Q_LORA_RANK + KV_LORA_RANK:].T
  kr_roped = jnp.concatenate([_rope_t(kr_t[:QK_ROPE_DIM], cos_t, sin_t), pad], axis=0).T
  kc_ref[...] = jnp.concatenate([c_kv.astype(BF16), kr_roped.astype(BF16)], axis=1)
  ct_ref[0] = c_kv.T.astype(BF16)

  scale = QK_HEAD_DIM ** -0.5 * math.log2(math.e)
  cq_t = c_q.T.astype(BF16)
  qn_t = _dot(wqnt_ref[...], cq_t)
  qr_t = _dot(wqrt_ref[...], cq_t)
  for hd in range(MLA_HEADS):
    qn_h = qn_t[hd * QK_NOPE_DIM:(hd + 1) * QK_NOPE_DIM].astype(BF16)
    q_lat = _dot(wuk_ref[hd], qn_h) * scale
    q_rope = _rope_t(qr_t[hd * QK_ROPE_DIM:(hd + 1) * QK_ROPE_DIM], cos_t, sin_t) * scale
    qt_ref[0, :, hd * rows:(hd + 1) * rows] = jnp.concatenate(
        [q_lat, q_rope, pad], axis=0).astype(BF16)


def _mla_proj(x, g, wdkv, qn, kvn, wqnt, wqrt, wuk, cos_t, sin_t):
  t, d = x.shape
  rows = ATTN_ROWS
  steps = t // rows
  row_spec = lambda width: pl.BlockSpec((rows, width), lambda i: (i, 0))
  table_spec = pl.BlockSpec((ROPE_HALF, rows), lambda i: (0, i))
  in_specs = [row_spec(d), _resident(g.shape), _resident(wdkv.shape), _resident(qn.shape),
              _resident(kvn.shape), _resident(wqnt.shape), _resident(wqrt.shape),
              _resident(wuk.shape), table_spec, table_spec]
  out_shape = [
      jax.ShapeDtypeStruct((steps, QK_PAD, MLA_HEADS * rows), BF16),
      jax.ShapeDtypeStruct((t, QK_PAD), BF16),
      jax.ShapeDtypeStruct((steps, KV_LORA_RANK, rows), BF16),
  ]
  out_specs = [
      pl.BlockSpec((1, QK_PAD, MLA_HEADS * rows), lambda i: (i, 0, 0)),
      row_spec(QK_PAD),
      pl.BlockSpec((1, KV_LORA_RANK, rows), lambda i: (i, 0, 0)),
  ]
  weights = sum(_nbytes(w.shape, w.dtype) for w in (wdkv, wqnt, wqrt, wuk))
  blocks = weights + 2 * (_nbytes((rows, d), F32) + 2 * _nbytes((ROPE_HALF, rows), F32)
                          + _nbytes((MLA_HEADS + 2, rows, QK_PAD), BF16))
  temps = 6 * _nbytes((rows, MLA_HEADS * QK_HEAD_DIM), F32)
  return pl.pallas_call(
      _mla_proj_kernel,
      grid=(steps,),
      in_specs=in_specs,
      out_specs=out_specs,
      out_shape=out_shape,
      compiler_params=pltpu.CompilerParams(
          dimension_semantics=("arbitrary",),
          vmem_limit_bytes=_vmem_limit(blocks, temps)),
      name="mla_proj",
  )(x, g, wdkv, qn, kvn, wqnt, wqrt, wuk, cos_t, sin_t)


def _mla_attn_kernel(qt_ref, kc_ref, ct_ref, o_ref, s0_ref, s1_ref, mx0_ref, mx1_ref,
                     m_ref, l_ref, acc_ref):
  qi = pl.program_id(1)
  tk = ct_ref.shape[3]
  tq = qt_ref.shape[3] // MLA_HEADS
  q_t = qt_ref[0, 0]
  bufs = ((s0_ref, mx0_ref), (s1_ref, mx1_ref))

  m_ref[...] = jnp.full(m_ref.shape, -jnp.inf, F32)
  l_ref[...] = jnp.zeros(l_ref.shape, F32)
  acc_ref[...] = jnp.zeros(acc_ref.shape, F32)

  def scores(kt, buf):
    s_ref, mx_ref = bufs[buf]
    k = kc_ref[0, pl.ds(pl.multiple_of(kt * tk, tk), tk), :]
    s = _dot(k, q_t)
    s_ref[...] = s
    mx_ref[...] = jnp.max(s, axis=0, keepdims=True)

  def softmax_pv(kt, buf, diagonal):
    s_ref, mx_ref = bufs[buf]
    v_t = ct_ref[0, kt]
    for hd in range(MLA_HEADS):
      cols = slice(hd * tq, (hd + 1) * tq)
      s = s_ref[:, cols]
      if diagonal:
        key = lax.broadcasted_iota(jnp.int32, s.shape, 0)
        qry = lax.broadcasted_iota(jnp.int32, s.shape, 1)
        s = jnp.where(key <= qry, s, -jnp.inf)
        mx = jnp.max(s, axis=0, keepdims=True)
      else:
        mx = mx_ref[:, cols]
      m_prev = m_ref[:, cols]
      m_new = jnp.maximum(m_prev, mx)
      alpha = jnp.exp2(m_prev - m_new)
      p = jnp.exp2(s - m_new)
      l_ref[:, cols] = alpha * l_ref[:, cols] + jnp.sum(p, axis=0, keepdims=True)
      acc_ref[:, cols] = alpha * acc_ref[:, cols] + _dot(v_t, p.astype(BF16))
      m_ref[:, cols] = m_new

  scores(0, 0)

  def body(j, carry):
    scores(2 * j + 1, 1)
    softmax_pv(2 * j, 0, False)
    scores(2 * j + 2, 0)
    softmax_pv(2 * j + 1, 1, False)
    return carry

  lax.fori_loop(0, qi // 2, body, 0)

  @pl.when(qi % 2 == 0)
  def _():
    softmax_pv(qi, 0, True)

  @pl.when(qi % 2 == 1)
  def _():
    scores(qi, 1)
    softmax_pv(qi - 1, 0, False)
    softmax_pv(qi, 1, True)

  o_t = acc_ref[...] / l_ref[...]
  for hd in range(MLA_HEADS):
    o_ref[0, :, hd * KV_LORA_RANK:(hd + 1) * KV_LORA_RANK] = (
        o_t[:, hd * tq:(hd + 1) * tq].T.astype(o_ref.dtype))


def _mla_attn(q_t, kc, ct, batch, seq):
  tq = tk = ATTN_ROWS
  nq = MLA_HEADS * tq
  q_tiles = seq // tq
  k_tiles = seq // tk
  qt4 = q_t.reshape(batch, q_tiles, QK_PAD, nq)
  kc3 = kc.reshape(batch, seq, QK_PAD)
  ct4 = ct.reshape(batch, k_tiles, KV_LORA_RANK, tk)
  out_w = MLA_HEADS * KV_LORA_RANK
  blocks = 2 * (_nbytes((QK_PAD, nq), BF16) + _nbytes((seq, QK_PAD), BF16)
                + _nbytes((KV_LORA_RANK, seq), BF16) + _nbytes((tq, out_w), BF16))
  scratch = _nbytes((2 * tk + KV_LORA_RANK + 32, nq), F32)
  temps = 2 * _nbytes((tk, nq), F32)
  return pl.pallas_call(
      _mla_attn_kernel,
      grid=(batch, q_tiles),
      in_specs=[
          pl.BlockSpec((1, 1, QK_PAD, nq), lambda b, i: (b, i, 0, 0)),
          pl.BlockSpec((1, seq, QK_PAD), lambda b, i: (b, 0, 0)),
          pl.BlockSpec((1, k_tiles, KV_LORA_RANK, tk), lambda b, i: (b, 0, 0, 0)),
      ],
      out_specs=pl.BlockSpec((1, tq, out_w), lambda b, i: (b, i, 0)),
      out_shape=jax.ShapeDtypeStruct((batch, seq, out_w), BF16),
      scratch_shapes=[pltpu.VMEM((tk, nq), F32), pltpu.VMEM((tk, nq), F32),
                      pltpu.VMEM((1, nq), F32), pltpu.VMEM((1, nq), F32),
                      pltpu.VMEM((1, nq), F32), pltpu.VMEM((1, nq), F32),
                      pltpu.VMEM((KV_LORA_RANK, nq), F32)],
      compiler_params=pltpu.CompilerParams(
          dimension_semantics=("arbitrary", "arbitrary"),
          vmem_limit_bytes=_vmem_limit(blocks + scratch, temps)),
      name="mla_attn",
  )(qt4, kc3, ct4)


def _mla_out_kernel(x_ref, ol_ref, wuv_ref, wo_ref, o_ref):
  ol = ol_ref[...]
  heads = []
  for hd in range(MLA_HEADS):
    heads.append(_dot(ol[:, hd * KV_LORA_RANK:(hd + 1) * KV_LORA_RANK],
                      wuv_ref[hd]).astype(BF16))
  o = jnp.concatenate(heads, axis=1)
  o_ref[...] = x_ref[...] + _dot(o, wo_ref[...])


def _mla_out(x, o_lat, wuv, wo):
  t, d = x.shape
  rows = MLP_ROWS
  blocks = (_nbytes(wuv.shape, BF16) + _nbytes(wo.shape, BF16)
            + 2 * (2 * _nbytes((rows, d), F32) + _nbytes((rows, o_lat.shape[1]), BF16)))
  temps = 3 * _nbytes((rows, d), F32)
  return pl.pallas_call(
      _mla_out_kernel,
      grid=(t // rows,),
      in_specs=[pl.BlockSpec((rows, d), lambda i: (i, 0)),
                pl.BlockSpec((rows, o_lat.shape[1]), lambda i: (i, 0)),
                _resident(wuv.shape), _resident(wo.shape)],
      out_specs=pl.BlockSpec((rows, d), lambda i: (i, 0)),
      out_shape=jax.ShapeDtypeStruct((t, d), F32),
      compiler_params=pltpu.CompilerParams(
          dimension_semantics=("arbitrary",),
          vmem_limit_bytes=_vmem_limit(blocks, temps)),
      name="mla_out",
  )(x, o_lat, wuv, wo)


def _sgu_kernel(x_ref, g_ref, win_ref, lng_ref, lnb_ref, wsp_ref, bsp_ref, wout_ref,
                o_ref, gated_ref):
  x = x_ref[...]
  rows = x.shape[0]
  width = lng_ref.shape[1]
  group_dim = width // SGU_GROUPS
  h = _rms(x, g_ref[...]).astype(BF16)
  z = _dot(h, win_ref[...])
  z = 0.5 * z * (1.0 + lax.erf(z * (0.5 ** 0.5)))
  u = z[:, :width]
  v = z[:, width:]
  mu = jnp.mean(v, axis=-1, keepdims=True)
  var = jnp.mean(jnp.square(v - mu), axis=-1, keepdims=True)
  v = ((v - mu) * lax.rsqrt(var + LN_EPS) * lng_ref[...] + lnb_ref[...]).astype(BF16)

  t_idx = lax.broadcasted_iota(jnp.int32, (SGU_CHUNK, SGU_CHUNK), 0)
  s_idx = lax.broadcasted_iota(jnp.int32, (SGU_CHUNK, SGU_CHUNK), 1)
  causal = s_idx <= t_idx
  for gi in range(SGU_GROUPS):
    w_g = jnp.where(causal, wsp_ref[gi], 0.0).astype(BF16)
    bias = jnp.broadcast_to(bsp_ref[:, gi:gi + 1], (SGU_CHUNK, group_dim))
    cols = slice(gi * group_dim, (gi + 1) * group_dim)
    for ci in range(rows // SGU_CHUNK):
      rws = slice(ci * SGU_CHUNK, (ci + 1) * SGU_CHUNK)
      mixed = _dot(w_g, v[rws, cols]) + bias
      gated_ref[rws, cols] = (u[rws, cols] * mixed).astype(BF16)
  o_ref[...] = x + _dot(gated_ref[...], wout_ref[...])


def _sgu(x, g, win, lng, lnb, wsp, bsp_t, wout):
  t, d = x.shape
  rows = MLP_ROWS
  width = lng.shape[1]
  blocks = (_nbytes(win.shape, BF16) + _nbytes(wout.shape, BF16) + _nbytes(wsp.shape, F32)
            + 4 * _nbytes((rows, d), F32) + _nbytes((rows, width), BF16))
  temps = 3 * _nbytes((rows, 2 * width), F32)
  return pl.pallas_call(
      _sgu_kernel,
      grid=(t // rows,),
      in_specs=[pl.BlockSpec((rows, d), lambda i: (i, 0)), _resident(g.shape),
                _resident(win.shape), _resident(lng.shape), _resident(lnb.shape),
                _resident(wsp.shape), _resident(bsp_t.shape), _resident(wout.shape)],
      out_specs=pl.BlockSpec((rows, d), lambda i: (i, 0)),
      out_shape=jax.ShapeDtypeStruct((t, d), F32),
      scratch_shapes=[pltpu.VMEM((rows, width), BF16)],
      compiler_params=pltpu.CompilerParams(
          dimension_semantics=("arbitrary",),
          vmem_limit_bytes=_vmem_limit(blocks, temps)),
      name="sgu_mixer",
  )(x, g, win, lng, lnb, wsp, bsp_t, wout)


def _ffn_kernel(x_ref, g_ref, wup_ref, wdn_ref, *rest, final_norm):
  o_ref = rest[-1]
  x = x_ref[...]
  h = _rms(x, g_ref[...]).astype(BF16)
  a = jnp.square(jnp.maximum(_dot(h, wup_ref[...]), 0.0)).astype(BF16)
  y = x + _dot(a, wdn_ref[...])
  if final_norm:
    y = _rms(y, rest[0][...])
  o_ref[...] = y


def _ffn(x, g, wup, wdn, final_g=None):
  t, d = x.shape
  rows = MLP_ROWS
  hidden = wup.shape[1]
  args = [x, g, wup, wdn]
  in_specs = [pl.BlockSpec((rows, d), lambda i: (i, 0)), _resident(g.shape),
              _resident(wup.shape), _resident(wdn.shape)]
  if final_g is not None:
    args.append(final_g)
    in_specs.append(_resident(final_g.shape))
  blocks = _nbytes(wup.shape, BF16) + _nbytes(wdn.shape, BF16) + 4 * _nbytes((rows, d), F32)
  temps = _nbytes((rows, hidden), F32) + _nbytes((rows, hidden), BF16) + 2 * _nbytes((rows, d), F32)
  return pl.pallas_call(
      functools.partial(_ffn_kernel, final_norm=final_g is not None),
      grid=(t // rows,),
      in_specs=in_specs,
      out_specs=pl.BlockSpec((rows, d), lambda i: (i, 0)),
      out_shape=jax.ShapeDtypeStruct((t, d), F32),
      compiler_params=pltpu.CompilerParams(
          dimension_semantics=("arbitrary",),
          vmem_limit_bytes=_vmem_limit(blocks, temps)),
      name="ffn",
  )(*args)


def kernel(x, positions, norm_mix, norm_ffn, final_norm, mla_w_dkv, mla_q_norm, mla_kv_norm, mla_w_uq, mla_w_ukv, mla_w_o, sgu_w_in, sgu_ln_g, sgu_ln_b, sgu_w_spatial, sgu_b_spatial, sgu_w_out, ffn_w_up, ffn_w_down):
  batch, seq, d = x.shape
  depth = norm_mix.shape[0]
  t = batch * seq
  assert seq % MLP_ROWS == 0 and seq % ATTN_ROWS == 0 and t % ROPE_TABLE_COLS == 0
  xf = x.reshape(t, d)

  inv_freq = ROPE_THETA ** (-jnp.arange(0, QK_ROPE_DIM, 2, dtype=F32) / QK_ROPE_DIM)
  cos_t, sin_t = _rope_tables(positions.astype(F32).reshape(1, t), inv_freq.reshape(ROPE_HALF, 1))

  row = lambda a: a.reshape(1, -1)
  for i in range(depth):
    j = i // 2
    if i % 2 == 0:
      wdkv = jnp.pad(mla_w_dkv[j], ((0, 0), (0, V7X_LANES - QK_ROPE_DIM))).astype(BF16)
      wuq = mla_w_uq[j].reshape(Q_LORA_RANK, MLA_HEADS, QK_HEAD_DIM)
      wqnt = wuq[:, :, :QK_NOPE_DIM].reshape(Q_LORA_RANK, -1).T.astype(BF16)
      wqrt = wuq[:, :, QK_NOPE_DIM:].reshape(Q_LORA_RANK, -1).T.astype(BF16)
      wukv = mla_w_ukv[j].reshape(KV_LORA_RANK, MLA_HEADS, QK_NOPE_DIM + V_HEAD_DIM)
      wuk = wukv[:, :, :QK_NOPE_DIM].transpose(1, 0, 2).astype(BF16)
      wuv = wukv[:, :, QK_NOPE_DIM:].transpose(1, 0, 2).astype(BF16)
      q_t, kc, ct = _mla_proj(xf, row(norm_mix[i]), wdkv, row(mla_q_norm[j]),
                              row(mla_kv_norm[j]), wqnt, wqrt, wuk, cos_t, sin_t)
      o_lat = _mla_attn(q_t, kc, ct, batch, seq)
      xf = _mla_out(xf, o_lat.reshape(t, -1), wuv, mla_w_o[j].astype(BF16))
    else:
      xf = _sgu(xf, row(norm_mix[i]), sgu_w_in[j].astype(BF16), row(sgu_ln_g[j]),
                row(sgu_ln_b[j]), sgu_w_spatial[j], sgu_b_spatial[j].T,
                sgu_w_out[j].astype(BF16))
    last = i == depth - 1
    xf = _ffn(xf, row(norm_ffn[i]), ffn_w_up[i].astype(BF16), ffn_w_down[i].astype(BF16),
              final_g=row(final_norm) if last else None)
  return xf.reshape(batch, seq, d)
```

```python
import functools
import math

import jax
import jax.numpy as jnp
from jax import lax
from jax.experimental import pallas as pl
from jax.experimental.pallas import tpu as pltpu

F32 = jnp.float32
BF16 = jnp.bfloat16

MLA_HEADS = 8
QK_NOPE_DIM = 128
QK_ROPE_DIM = 64
QK_HEAD_DIM = QK_NOPE_DIM + QK_ROPE_DIM
V_HEAD_DIM = 128
Q_LORA_RANK = 256
KV_LORA_RANK = 128
ROPE_THETA = 10000.0
SGU_CHUNK = 128
SGU_GROUPS = 8
NORM_EPS = 1e-6
LN_EPS = 1e-5

V7X_LANES = 128
V7X_BF16_SUBLANES = 16
V7X_VMEM_BYTES = 64 * 1024 * 1024
V7X_VMEM_RESERVE_BYTES = 8 * 1024 * 1024

QK_PAD = KV_LORA_RANK + V7X_LANES
ROPE_HALF = QK_ROPE_DIM // 2
V_ROWS = KV_LORA_RANK + V7X_BF16_SUBLANES

ROPE_TABLE_COLS = 2048
ATTN_Q = 512
ATTN_K = 256
MLP_ROWS = 512


def _vmem_limit(block_bytes, temp_bytes):
  want = block_bytes + temp_bytes + V7X_VMEM_RESERVE_BYTES
  return int(min(want, V7X_VMEM_BYTES - V7X_VMEM_RESERVE_BYTES))


def _nbytes(shape, dtype):
  return math.prod(shape) * jnp.dtype(dtype).itemsize


def _dot(a, b):
  return jnp.dot(a, b, preferred_element_type=F32)


def _rms(x, g):
  return x * lax.rsqrt(jnp.mean(x * x, axis=-1, keepdims=True) + NORM_EPS) * g


def _resident(shape):
  nd = len(shape)
  return pl.BlockSpec(shape, lambda *_: (0,) * nd, pipeline_mode=pl.Buffered(1))


def _rope_table_kernel(pos_ref, freq_ref, cos_ref, sin_ref):
  ang = freq_ref[...] * pos_ref[...]
  cos_ref[...] = jnp.cos(ang)
  sin_ref[...] = jnp.sin(ang)


def _rope_tables(pos_row, freq_col):
  t = pos_row.shape[1]
  cols = ROPE_TABLE_COLS
  out = jax.ShapeDtypeStruct((ROPE_HALF, t), F32)
  return pl.pallas_call(
      _rope_table_kernel,
      grid=(t // cols,),
      in_specs=[pl.BlockSpec((1, cols), lambda i: (0, i)),
                pl.BlockSpec((ROPE_HALF, 1), lambda i: (0, 0))],
      out_specs=[pl.BlockSpec((ROPE_HALF, cols), lambda i: (0, i))] * 2,
      out_shape=[out, out],
      name="rope_tables",
  )(pos_row, freq_col)


def _rope_t(x_t, cos_t, sin_t):
  x1, x2 = x_t[:ROPE_HALF], x_t[ROPE_HALF:]
  return jnp.concatenate([x1 * cos_t - x2 * sin_t, x2 * cos_t + x1 * sin_t], axis=0)


def _mla_proj_kernel(x_ref, g_ref, wdkv_ref, qn_ref, kvn_ref, wqnt_ref, wqrt_ref,
                     wuk_ref, cos_ref, sin_ref, qt_ref, kc_ref, ct_ref):
  rows = x_ref.shape[0]
  h = _rms(x_ref[...], g_ref[...]).astype(BF16)
  lat = _dot(h, wdkv_ref[...])
  c_q = _rms(lat[:, :Q_LORA_RANK], qn_ref[...])
  c_kv = _rms(lat[:, Q_LORA_RANK:Q_LORA_RANK + KV_LORA_RANK], kvn_ref[...])
  cos_t = cos_ref[...]
  sin_t = sin_ref[...]
  pad = jnp.zeros((V7X_LANES - QK_ROPE_DIM, rows), F32)

  kr_t = lat[:, Q_LORA_RANK + KV_LORA_RANK:].T
  kr_roped = jnp.concatenate([_rope_t(kr_t[:QK_ROPE_DIM], cos_t, sin_t), pad], axis=0).T
  kc_ref[...] = jnp.concatenate([c_kv.astype(BF16), kr_roped.astype(BF16)], axis=1)
  ct_ref[0] = jnp.concatenate(
      [c_kv.T, jnp.ones((V_ROWS - KV_LORA_RANK, rows), F32)], axis=0).astype(BF16)

  scale = QK_HEAD_DIM ** -0.5 * math.log2(math.e)
  cq_t = c_q.T.astype(BF16)
  qn_t = _dot(wqnt_ref[...], cq_t)
  qr_t = _dot(wqrt_ref[...], cq_t)
  for hd in range(MLA_HEADS):
    qn_h = qn_t[hd * QK_NOPE_DIM:(hd + 1) * QK_NOPE_DIM].astype(BF16)
    q_lat = _dot(wuk_ref[hd], qn_h) * scale
    q_rope = _rope_t(qr_t[hd * QK_ROPE_DIM:(hd + 1) * QK_ROPE_DIM], cos_t, sin_t) * scale
    qt_ref[0, :, hd * rows:(hd + 1) * rows] = jnp.concatenate(
        [q_lat, q_rope, pad], axis=0).astype(BF16)


def _mla_proj(x, g, wdkv, qn, kvn, wqnt, wqrt, wuk, cos_t, sin_t):
  t, d = x.shape
  rows = ATTN_Q
  steps = t // rows
  row_spec = lambda width: pl.BlockSpec((rows, width), lambda i: (i, 0))
  table_spec = pl.BlockSpec((ROPE_HALF, rows), lambda i: (0, i))
  in_specs = [row_spec(d), _resident(g.shape), _resident(wdkv.shape), _resident(qn.shape),
              _resident(kvn.shape), _resident(wqnt.shape), _resident(wqrt.shape),
              _resident(wuk.shape), table_spec, table_spec]
  out_shape = [
      jax.ShapeDtypeStruct((steps, QK_PAD, MLA_HEADS * rows), BF16),
      jax.ShapeDtypeStruct((t, QK_PAD), BF16),
      jax.ShapeDtypeStruct((steps, V_ROWS, rows), BF16),
  ]
  out_specs = [
      pl.BlockSpec((1, QK_PAD, MLA_HEADS * rows), lambda i: (i, 0, 0)),
      row_spec(QK_PAD),
      pl.BlockSpec((1, V_ROWS, rows), lambda i: (i, 0, 0)),
  ]
  weights = sum(_nbytes(w.shape, w.dtype) for w in (wdkv, wqnt, wqrt, wuk))
  blocks = weights + 2 * (_nbytes((rows, d), F32) + 2 * _nbytes((ROPE_HALF, rows), F32)
                          + _nbytes((MLA_HEADS + 2, rows, QK_PAD), BF16))
  temps = 6 * _nbytes((rows, MLA_HEADS * QK_HEAD_DIM), F32)
  return pl.pallas_call(
      _mla_proj_kernel,
      grid=(steps,),
      in_specs=in_specs,
      out_specs=out_specs,
      out_shape=out_shape,
      compiler_params=pltpu.CompilerParams(
          dimension_semantics=("arbitrary",),
          vmem_limit_bytes=_vmem_limit(blocks, temps)),
      name="mla_proj",
  )(x, g, wdkv, qn, kvn, wqnt, wqrt, wuk, cos_t, sin_t)


_FULL, _DIAG, _SKIP = "full", "diag", "skip"


def _mla_attn_kernel(qt_ref, kc_ref, ct_ref, o_ref, s0_ref, s1_ref, mx0_ref, mx1_ref,
                     m_ref, acc_ref):
  qi = pl.program_id(1)
  tk = ATTN_K
  tq = qt_ref.shape[3] // MLA_HEADS
  units = qt_ref.shape[3] // tk
  bufs = ((s0_ref, mx0_ref), (s1_ref, mx1_ref))

  m_ref[...] = jnp.full(m_ref.shape, -jnp.inf, F32)
  acc_ref[...] = jnp.zeros(acc_ref.shape, F32)

  def keys(kt):
    return kc_ref[0, pl.ds(pl.multiple_of(kt * tk, tk), tk), :]

  def scores(kt, buf):
    s_ref, mx_ref = bufs[buf]
    s = _dot(keys(kt), qt_ref[0, 0])
    s_ref[...] = s
    mx_ref[...] = jnp.max(s, axis=0, keepdims=True)

  def scores_upper_half(kt, buf):
    s_ref, _ = bufs[buf]
    k = keys(kt)
    for u in range(1, units, 2):
      cols = slice(u * tk, (u + 1) * tk)
      s_ref[:, cols] = _dot(k, qt_ref[0, 0, :, cols])

  def softmax_pv(sb, half, buf, mode_of_unit):
    s_ref, mx_ref = bufs[buf]
    v_t = ct_ref[0, sb, :, half * tk:(half + 1) * tk]
    for u in range(units):
      mode = mode_of_unit(u)
      if mode == _SKIP:
        continue
      cols = slice(u * tk, (u + 1) * tk)
      s = s_ref[:, cols]
      if mode == _DIAG:
        key = lax.broadcasted_iota(jnp.int32, s.shape, 0)
        qry = lax.broadcasted_iota(jnp.int32, s.shape, 1)
        s = jnp.where(key <= qry, s, -jnp.inf)
        mx = jnp.max(s, axis=0, keepdims=True)
      else:
        mx = mx_ref[:, cols]
      m_prev = m_ref[:, cols]
      m_new = jnp.maximum(m_prev, mx)
      alpha = jnp.exp2(m_prev - m_new)
      p = jnp.exp2(s - m_new).astype(BF16)
      acc_ref[:, cols] = alpha * acc_ref[:, cols] + _dot(v_t, p)
      m_ref[:, cols] = m_new

  scores(0, 0)

  def body(j, carry):
    scores(2 * j + 1, 1)
    softmax_pv(j, 0, 0, lambda u: _FULL)
    scores(2 * j + 2, 0)
    softmax_pv(j, 1, 1, lambda u: _FULL)
    return carry

  lax.fori_loop(0, qi, body, 0)
  scores_upper_half(2 * qi + 1, 1)
  softmax_pv(qi, 0, 0, lambda u: _DIAG if u % 2 == 0 else _FULL)
  softmax_pv(qi, 1, 1, lambda u: _SKIP if u % 2 == 0 else _DIAG)

  o_t = acc_ref[:KV_LORA_RANK, :] / acc_ref[KV_LORA_RANK:KV_LORA_RANK + 1, :]
  for hd in range(MLA_HEADS):
    o_ref[0, :, hd * KV_LORA_RANK:(hd + 1) * KV_LORA_RANK] = (
        o_t[:, hd * tq:(hd + 1) * tq].T.astype(o_ref.dtype))


def _mla_attn(q_t, kc, ct, batch, seq):
  tq, tk = ATTN_Q, ATTN_K
  assert tq == 2 * tk
  nq = MLA_HEADS * tq
  q_tiles = seq // tq
  qt4 = q_t.reshape(batch, q_tiles, QK_PAD, nq)
  kc3 = kc.reshape(batch, seq, QK_PAD)
  ct4 = ct.reshape(batch, q_tiles, V_ROWS, tq)
  out_w = MLA_HEADS * KV_LORA_RANK
  blocks = 2 * (_nbytes((QK_PAD, nq), BF16) + _nbytes((seq, QK_PAD), BF16)
                + _nbytes((V_ROWS, seq), BF16) + _nbytes((tq, out_w), BF16))
  scratch = _nbytes((2 * tk + V_ROWS + 24, nq), F32)
  temps = 2 * _nbytes((tk, nq), F32)
  return pl.pallas_call(
      _mla_attn_kernel,
      grid=(batch, q_tiles),
      in_specs=[
          pl.BlockSpec((1, 1, QK_PAD, nq), lambda b, i: (b, i, 0, 0)),
          pl.BlockSpec((1, seq, QK_PAD), lambda b, i: (b, 0, 0)),
          pl.BlockSpec((1, q_tiles, V_ROWS, tq), lambda b, i: (b, 0, 0, 0)),
      ],
      out_specs=pl.BlockSpec((1, tq, out_w), lambda b, i: (b, i, 0)),
      out_shape=jax.ShapeDtypeStruct((batch, seq, out_w), BF16),
      scratch_shapes=[pltpu.VMEM((tk, nq), F32), pltpu.VMEM((tk, nq), F32),
                      pltpu.VMEM((1, nq), F32), pltpu.VMEM((1, nq), F32),
                      pltpu.VMEM((1, nq), F32), pltpu.VMEM((V_ROWS, nq), F32)],
      compiler_params=pltpu.CompilerParams(
          dimension_semantics=("arbitrary", "arbitrary"),
          vmem_limit_bytes=_vmem_limit(blocks + scratch, temps)),
      name="mla_attn",
  )(qt4, kc3, ct4)


def _sgu_kernel(x_ref, g_ref, win_ref, lng_ref, lnb_ref, wsp_ref, bsp_ref, wout_ref,
                o_ref, gated_ref):
  x = x_ref[...]
  rows = x.shape[0]
  width = lng_ref.shape[1]
  group_dim = width // SGU_GROUPS
  h = _rms(x, g_ref[...]).astype(BF16)
  z = _dot(h, win_ref[...])
  z = 0.5 * z * (1.0 + lax.erf(z * (0.5 ** 0.5)))
  u = z[:, :width]
  v = z[:, width:]
  mu = jnp.mean(v, axis=-1, keepdims=True)
  var = jnp.mean(jnp.square(v - mu), axis=-1, keepdims=True)
  v = ((v - mu) * lax.rsqrt(var + LN_EPS) * lng_ref[...] + lnb_ref[...]).astype(BF16)

  t_idx = lax.broadcasted_iota(jnp.int32, (SGU_CHUNK, SGU_CHUNK), 0)
  s_idx = lax.broadcasted_iota(jnp.int32, (SGU_CHUNK, SGU_CHUNK), 1)
  causal = s_idx <= t_idx
  for gi in range(SGU_GROUPS):
    w_g = jnp.where(causal, wsp_ref[gi], 0.0).astype(BF16)
    bias = jnp.broadcast_to(bsp_ref[:, gi:gi + 1], (SGU_CHUNK, group_dim))
    cols = slice(gi * group_dim, (gi + 1) * group_dim)
    for ci in range(rows // SGU_CHUNK):
      rws = slice(ci * SGU_CHUNK, (ci + 1) * SGU_CHUNK)
      mixed = _dot(w_g, v[rws, cols]) + bias
      gated_ref[rws, cols] = (u[rws, cols] * mixed).astype(BF16)
  o_ref[...] = x + _dot(gated_ref[...], wout_ref[...])


def _sgu(x, g, win, lng, lnb, wsp, bsp_t, wout):
  t, d = x.shape
  rows = MLP_ROWS
  width = lng.shape[1]
  blocks = (_nbytes(win.shape, BF16) + _nbytes(wout.shape, BF16) + _nbytes(wsp.shape, F32)
            + 4 * _nbytes((rows, d), F32) + _nbytes((rows, width), BF16))
  temps = 3 * _nbytes((rows, 2 * width), F32)
  return pl.pallas_call(
      _sgu_kernel,
      grid=(t // rows,),
      in_specs=[pl.BlockSpec((rows, d), lambda i: (i, 0)), _resident(g.shape),
                _resident(win.shape), _resident(lng.shape), _resident(lnb.shape),
                _resident(wsp.shape), _resident(bsp_t.shape), _resident(wout.shape)],
      out_specs=pl.BlockSpec((rows, d), lambda i: (i, 0)),
      out_shape=jax.ShapeDtypeStruct((t, d), F32),
      scratch_shapes=[pltpu.VMEM((rows, width), BF16)],
      compiler_params=pltpu.CompilerParams(
          dimension_semantics=("arbitrary",),
          vmem_limit_bytes=_vmem_limit(blocks, temps)),
      name="sgu_mixer",
  )(x, g, win, lng, lnb, wsp, bsp_t, wout)


def _ffn_kernel(*refs, mla_out, final_norm):
  refs = list(refs)
  o_ref = refs.pop()
  x = refs.pop(0)[...]
  if mla_out:
    ol_ref, wuv_ref, wo_ref = refs[:3]
    del refs[:3]
    ol = ol_ref[...]
    heads = [_dot(ol[:, hd * KV_LORA_RANK:(hd + 1) * KV_LORA_RANK], wuv_ref[hd]).astype(BF16)
             for hd in range(MLA_HEADS)]
    x = x + _dot(jnp.concatenate(heads, axis=1), wo_ref[...])
  g_ref, wup_ref, wdn_ref = refs[:3]
  h = _rms(x, g_ref[...]).astype(BF16)
  a = jnp.square(jnp.maximum(_dot(h, wup_ref[...]), 0.0)).astype(BF16)
  y = x + _dot(a, wdn_ref[...])
  if final_norm:
    y = _rms(y, refs[3][...])
  o_ref[...] = y


def _ffn(x, g, wup, wdn, mla_out=None, final_g=None):
  t, d = x.shape
  rows = MLP_ROWS
  hidden = wup.shape[1]
  row_spec = lambda width: pl.BlockSpec((rows, width), lambda i: (i, 0))
  args = [x]
  in_specs = [row_spec(d)]
  blocks = 4 * _nbytes((rows, d), F32)
  if mla_out is not None:
    o_lat, wuv, wo = mla_out
    args += [o_lat, wuv, wo]
    in_specs += [row_spec(o_lat.shape[1]), _resident(wuv.shape), _resident(wo.shape)]
    blocks += (2 * _nbytes((rows, o_lat.shape[1]), BF16) + _nbytes(wuv.shape, BF16)
               + _nbytes(wo.shape, BF16))
  args += [g, wup, wdn]
  in_specs += [_resident(g.shape), _resident(wup.shape), _resident(wdn.shape)]
  blocks += _nbytes(wup.shape, BF16) + _nbytes(wdn.shape, BF16)
  if final_g is not None:
    args.append(final_g)
    in_specs.append(_resident(final_g.shape))
  temps = _nbytes((rows, hidden), F32) + _nbytes((rows, hidden), BF16) + 2 * _nbytes((rows, d), F32)
  return pl.pallas_call(
      functools.partial(_ffn_kernel, mla_out=mla_out is not None, final_norm=final_g is not None),
      grid=(t // rows,),
      in_specs=in_specs,
      out_specs=row_spec(d),
      out_shape=jax.ShapeDtypeStruct((t, d), F32),
      compiler_params=pltpu.CompilerParams(
          dimension_semantics=("arbitrary",),
          vmem_limit_bytes=_vmem_limit(blocks, temps)),
      name="ffn",
  )(*args)


def kernel(x, positions, norm_mix, norm_ffn, final_norm, mla_w_dkv, mla_q_norm, mla_kv_norm, mla_w_uq, mla_w_ukv, mla_w_o, sgu_w_in, sgu_ln_g, sgu_ln_b, sgu_w_spatial, sgu_b_spatial, sgu_w_out, ffn_w_up, ffn_w_down):
  batch, seq, d = x.shape
  depth = norm_mix.shape[0]
  t = batch * seq
  assert seq % MLP_ROWS == 0 and seq % ATTN_Q == 0 and t % ROPE_TABLE_COLS == 0
  xf = x.reshape(t, d)

  inv_freq = ROPE_THETA ** (-jnp.arange(0, QK_ROPE_DIM, 2, dtype=F32) / QK_ROPE_DIM)
  cos_t, sin_t = _rope_tables(positions.astype(F32).reshape(1, t), inv_freq.reshape(ROPE_HALF, 1))

  row = lambda a: a.reshape(1, -1)
  for i in range(depth):
    j = i // 2
    mla_out = None
    if i % 2 == 0:
      wdkv = jnp.pad(mla_w_dkv[j], ((0, 0), (0, V7X_LANES - QK_ROPE_DIM))).astype(BF16)
      wuq = mla_w_uq[j].reshape(Q_LORA_RANK, MLA_HEADS, QK_HEAD_DIM)
      wqnt = wuq[:, :, :QK_NOPE_DIM].reshape(Q_LORA_RANK, -1).T.astype(BF16)
      wqrt = wuq[:, :, QK_NOPE_DIM:].reshape(Q_LORA_RANK, -1).T.astype(BF16)
      wukv = mla_w_ukv[j].reshape(KV_LORA_RANK, MLA_HEADS, QK_NOPE_DIM + V_HEAD_DIM)
      wuk = wukv[:, :, :QK_NOPE_DIM].transpose(1, 0, 2).astype(BF16)
      wuv = wukv[:, :, QK_NOPE_DIM:].transpose(1, 0, 2).astype(BF16)
      q_t, kc, ct = _mla_proj(xf, row(norm_mix[i]), wdkv, row(mla_q_norm[j]),
                              row(mla_kv_norm[j]), wqnt, wqrt, wuk, cos_t, sin_t)
      o_lat = _mla_attn(q_t, kc, ct, batch, seq)
      mla_out = (o_lat.reshape(t, -1), wuv, mla_w_o[j].astype(BF16))
    else:
      xf = _sgu(xf, row(norm_mix[i]), sgu_w_in[j].astype(BF16), row(sgu_ln_g[j]),
                row(sgu_ln_b[j]), sgu_w_spatial[j], sgu_b_spatial[j].T,
                sgu_w_out[j].astype(BF16))
    last = i == depth - 1
    xf = _ffn(xf, row(norm_ffn[i]), ffn_w_up[i].astype(BF16), ffn_w_down[i].astype(BF16),
              mla_out=mla_out, final_g=row(final_norm) if last else None)
  return xf.reshape(batch, seq, d)
```

```python
import functools
import math

import jax
import jax.numpy as jnp
from jax import lax
from jax.experimental import pallas as pl
from jax.experimental.pallas import tpu as pltpu

F32 = jnp.float32
BF16 = jnp.bfloat16

MLA_HEADS = 8
QK_NOPE_DIM = 128
QK_ROPE_DIM = 64
QK_HEAD_DIM = QK_NOPE_DIM + QK_ROPE_DIM
V_HEAD_DIM = 128
Q_LORA_RANK = 256
KV_LORA_RANK = 128
ROPE_THETA = 10000.0
SGU_CHUNK = 128
SGU_GROUPS = 8
NORM_EPS = 1e-6
LN_EPS = 1e-5

V7X_LANES = 128
V7X_BF16_SUBLANES = 16
V7X_VMEM_BYTES = 64 * 1024 * 1024
V7X_VMEM_RESERVE_BYTES = 8 * 1024 * 1024

QK_PAD = KV_LORA_RANK + V7X_LANES
ROPE_HALF = QK_ROPE_DIM // 2
V_ROWS = KV_LORA_RANK + V7X_BF16_SUBLANES

ROPE_TABLE_COLS = 2048
ATTN_Q = 512
ATTN_K = 256
MLP_ROWS = 512


def _vmem_limit(block_bytes, temp_bytes):
  want = block_bytes + temp_bytes + V7X_VMEM_RESERVE_BYTES
  return int(min(want, V7X_VMEM_BYTES - V7X_VMEM_RESERVE_BYTES))


def _nbytes(shape, dtype):
  return math.prod(shape) * jnp.dtype(dtype).itemsize


def _dot(a, b):
  return jnp.dot(a, b, preferred_element_type=F32)


def _rms(x, g):
  return x * lax.rsqrt(jnp.mean(x * x, axis=-1, keepdims=True) + NORM_EPS) * g


def _resident(shape):
  nd = len(shape)
  return pl.BlockSpec(shape, lambda *_: (0,) * nd, pipeline_mode=pl.Buffered(1))


def _layer_resident(stacked, layer):
  nd = stacked.ndim
  return pl.BlockSpec((1,) + stacked.shape[1:], lambda *_: (layer,) + (0,) * (nd - 1),
                      pipeline_mode=pl.Buffered(1))


def _rope_table_kernel(pos_ref, freq_ref, cos_ref, sin_ref):
  ang = freq_ref[...] * pos_ref[...]
  cos_ref[...] = jnp.cos(ang)
  sin_ref[...] = jnp.sin(ang)


def _rope_tables(pos_row, freq_col):
  t = pos_row.shape[1]
  cols = ROPE_TABLE_COLS
  out = jax.ShapeDtypeStruct((ROPE_HALF, t), F32)
  return pl.pallas_call(
      _rope_table_kernel,
      grid=(t // cols,),
      in_specs=[pl.BlockSpec((1, cols), lambda i: (0, i)),
                pl.BlockSpec((ROPE_HALF, 1), lambda i: (0, 0))],
      out_specs=[pl.BlockSpec((ROPE_HALF, cols), lambda i: (0, i))] * 2,
      out_shape=[out, out],
      name="rope_tables",
  )(pos_row, freq_col)


def _rope_t(x_t, cos_t, sin_t):
  x1, x2 = x_t[:ROPE_HALF], x_t[ROPE_HALF:]
  return jnp.concatenate([x1 * cos_t - x2 * sin_t, x2 * cos_t + x1 * sin_t], axis=0)


def _mla_proj_kernel(x_ref, g_ref, wdkv_ref, qn_ref, kvn_ref, wqnt_ref, wqrt_ref,
                     wuk_ref, cos_ref, sin_ref, qt_ref, kc_ref, ct_ref):
  rows = x_ref.shape[0]
  h = _rms(x_ref[...], g_ref[...]).astype(BF16)
  lat = _dot(h, wdkv_ref[...])
  c_q = _rms(lat[:, :Q_LORA_RANK], qn_ref[...])
  c_kv = _rms(lat[:, Q_LORA_RANK:Q_LORA_RANK + KV_LORA_RANK], kvn_ref[...])
  cos_t = cos_ref[...]
  sin_t = sin_ref[...]
  pad = jnp.zeros((V7X_LANES - QK_ROPE_DIM, rows), F32)

  kr_t = lat[:, Q_LORA_RANK + KV_LORA_RANK:].T
  kr_roped = jnp.concatenate([_rope_t(kr_t[:QK_ROPE_DIM], cos_t, sin_t), pad], axis=0).T
  kc_ref[...] = jnp.concatenate([c_kv.astype(BF16), kr_roped.astype(BF16)], axis=1)
  ct_ref[0] = jnp.concatenate(
      [c_kv.T, jnp.ones((V_ROWS - KV_LORA_RANK, rows), F32)], axis=0).astype(BF16)

  scale = QK_HEAD_DIM ** -0.5 * math.log2(math.e)
  cq_t = c_q.T.astype(BF16)
  qn_t = _dot(wqnt_ref[...], cq_t)
  qr_t = _dot(wqrt_ref[...], cq_t)
  for hd in range(MLA_HEADS):
    qn_h = qn_t[hd * QK_NOPE_DIM:(hd + 1) * QK_NOPE_DIM].astype(BF16)
    q_lat = _dot(wuk_ref[hd], qn_h) * scale
    q_rope = _rope_t(qr_t[hd * QK_ROPE_DIM:(hd + 1) * QK_ROPE_DIM], cos_t, sin_t) * scale
    qt_ref[0, :, hd * rows:(hd + 1) * rows] = jnp.concatenate(
        [q_lat, q_rope, pad], axis=0).astype(BF16)


def _mla_proj(x, g, wdkv, qn, kvn, wqnt, wqrt, wuk, cos_t, sin_t):
  t, d = x.shape
  rows = ATTN_Q
  steps = t // rows
  row_spec = lambda width: pl.BlockSpec((rows, width), lambda i: (i, 0))
  table_spec = pl.BlockSpec((ROPE_HALF, rows), lambda i: (0, i))
  in_specs = [row_spec(d), _resident(g.shape), _resident(wdkv.shape), _resident(qn.shape),
              _resident(kvn.shape), _resident(wqnt.shape), _resident(wqrt.shape),
              _resident(wuk.shape), table_spec, table_spec]
  out_shape = [
      jax.ShapeDtypeStruct((steps, QK_PAD, MLA_HEADS * rows), BF16),
      jax.ShapeDtypeStruct((t, QK_PAD), BF16),
      jax.ShapeDtypeStruct((steps, V_ROWS, rows), BF16),
  ]
  out_specs = [
      pl.BlockSpec((1, QK_PAD, MLA_HEADS * rows), lambda i: (i, 0, 0)),
      row_spec(QK_PAD),
      pl.BlockSpec((1, V_ROWS, rows), lambda i: (i, 0, 0)),
  ]
  weights = sum(_nbytes(w.shape, w.dtype) for w in (wdkv, wqnt, wqrt, wuk))
  blocks = weights + 2 * (_nbytes((rows, d), F32) + 2 * _nbytes((ROPE_HALF, rows), F32)
                          + _nbytes((MLA_HEADS + 2, rows, QK_PAD), BF16))
  temps = 6 * _nbytes((rows, MLA_HEADS * QK_HEAD_DIM), F32)
  return pl.pallas_call(
      _mla_proj_kernel,
      grid=(steps,),
      in_specs=in_specs,
      out_specs=out_specs,
      out_shape=out_shape,
      compiler_params=pltpu.CompilerParams(
          dimension_semantics=("arbitrary",),
          vmem_limit_bytes=_vmem_limit(blocks, temps)),
      name="mla_proj",
  )(x, g, wdkv, qn, kvn, wqnt, wqrt, wuk, cos_t, sin_t)


_FULL, _DIAG, _SKIP = "full", "diag", "skip"


def _mla_attn_kernel(qt_ref, kc_ref, ct_ref, o_ref, s0_ref, s1_ref, mx0_ref, mx1_ref,
                     m_ref, acc_ref):
  qi = pl.program_id(1)
  tk = ATTN_K
  tq = qt_ref.shape[3] // MLA_HEADS
  units = qt_ref.shape[3] // tk
  bufs = ((s0_ref, mx0_ref), (s1_ref, mx1_ref))

  m_ref[...] = jnp.full(m_ref.shape, -jnp.inf, F32)
  acc_ref[...] = jnp.zeros(acc_ref.shape, F32)

  def keys(kt):
    return kc_ref[0, pl.ds(pl.multiple_of(kt * tk, tk), tk), :]

  def score_unit(k, buf, u):
    s_ref, mx_ref = bufs[buf]
    cols = slice(u * tk, (u + 1) * tk)
    s = _dot(k, qt_ref[0, 0, :, cols])
    s_ref[:, cols] = s
    mx_ref[:, cols] = jnp.max(s, axis=0, keepdims=True)

  def softmax_unit(v_t, buf, u, mode):
    s_ref, mx_ref = bufs[buf]
    cols = slice(u * tk, (u + 1) * tk)
    s = s_ref[:, cols]
    if mode == _DIAG:
      key = lax.broadcasted_iota(jnp.int32, s.shape, 0)
      qry = lax.broadcasted_iota(jnp.int32, s.shape, 1)
      s = jnp.where(key <= qry, s, -jnp.inf)
      mx = jnp.max(s, axis=0, keepdims=True)
    else:
      mx = mx_ref[:, cols]
    m_prev = m_ref[:, cols]
    m_new = jnp.maximum(m_prev, mx)
    alpha = jnp.exp2(m_prev - m_new)
    p = jnp.exp2(s - m_new).astype(BF16)
    acc_ref[:, cols] = alpha * acc_ref[:, cols] + _dot(v_t, p)
    m_ref[:, cols] = m_new

  def step(next_kt, next_units, sb, half, buf, mode_of_unit):
    k = None if next_kt is None else keys(next_kt)
    v_t = None if sb is None else ct_ref[0, sb, :, half * tk:(half + 1) * tk]
    for u in range(units):
      if k is not None and u in next_units:
        score_unit(k, 1 - buf, u)
      if sb is not None and mode_of_unit(u) != _SKIP:
        softmax_unit(v_t, buf, u, mode_of_unit(u))

  all_units = range(units)
  step(0, all_units, None, None, 1, None)

  def body(j, carry):
    step(2 * j + 1, all_units, j, 0, 0, lambda u: _FULL)
    step(2 * j + 2, all_units, j, 1, 1, lambda u: _FULL)
    return carry

  lax.fori_loop(0, qi, body, 0)
  step(2 * qi + 1, range(1, units, 2), qi, 0, 0, lambda u: _DIAG if u % 2 == 0 else _FULL)
  step(None, (), qi, 1, 1, lambda u: _SKIP if u % 2 == 0 else _DIAG)

  o_t = acc_ref[:KV_LORA_RANK, :] / acc_ref[KV_LORA_RANK:KV_LORA_RANK + 1, :]
  for hd in range(MLA_HEADS):
    o_ref[0, :, hd * KV_LORA_RANK:(hd + 1) * KV_LORA_RANK] = (
        o_t[:, hd * tq:(hd + 1) * tq].T.astype(o_ref.dtype))


def _mla_attn(q_t, kc, ct, batch, seq):
  tq, tk = ATTN_Q, ATTN_K
  assert tq == 2 * tk
  nq = MLA_HEADS * tq
  q_tiles = seq // tq
  qt4 = q_t.reshape(batch, q_tiles, QK_PAD, nq)
  kc3 = kc.reshape(batch, seq, QK_PAD)
  ct4 = ct.reshape(batch, q_tiles, V_ROWS, tq)
  out_w = MLA_HEADS * KV_LORA_RANK
  blocks = 2 * (_nbytes((QK_PAD, nq), BF16) + _nbytes((seq, QK_PAD), BF16)
                + _nbytes((V_ROWS, seq), BF16) + _nbytes((tq, out_w), BF16))
  scratch = _nbytes((2 * tk + V_ROWS + 24, nq), F32)
  temps = 2 * _nbytes((tk, nq), F32)
  return pl.pallas_call(
      _mla_attn_kernel,
      grid=(batch, q_tiles),
      in_specs=[
          pl.BlockSpec((1, 1, QK_PAD, nq), lambda b, i: (b, i, 0, 0)),
          pl.BlockSpec((1, seq, QK_PAD), lambda b, i: (b, 0, 0)),
          pl.BlockSpec((1, q_tiles, V_ROWS, tq), lambda b, i: (b, 0, 0, 0)),
      ],
      out_specs=pl.BlockSpec((1, tq, out_w), lambda b, i: (b, i, 0)),
      out_shape=jax.ShapeDtypeStruct((batch, seq, out_w), BF16),
      scratch_shapes=[pltpu.VMEM((tk, nq), F32), pltpu.VMEM((tk, nq), F32),
                      pltpu.VMEM((1, nq), F32), pltpu.VMEM((1, nq), F32),
                      pltpu.VMEM((1, nq), F32), pltpu.VMEM((V_ROWS, nq), F32)],
      compiler_params=pltpu.CompilerParams(
          dimension_semantics=("arbitrary", "arbitrary"),
          vmem_limit_bytes=_vmem_limit(blocks + scratch, temps)),
      name="mla_attn",
  )(qt4, kc3, ct4)


def _sgu_kernel(x_ref, g_ref, win_ref, lng_ref, lnb_ref, wsp_ref, bsp_ref, wout_ref,
                o_ref, gated_ref):
  x = x_ref[...]
  rows = x.shape[0]
  width = lng_ref.shape[1]
  group_dim = width // SGU_GROUPS
  h = _rms(x, g_ref[...]).astype(BF16)
  z = _dot(h, win_ref[0])
  z = 0.5 * z * (1.0 + lax.erf(z * (0.5 ** 0.5)))
  u = z[:, :width]
  v = z[:, width:]
  mu = jnp.mean(v, axis=-1, keepdims=True)
  var = jnp.mean(jnp.square(v - mu), axis=-1, keepdims=True)
  v = ((v - mu) * lax.rsqrt(var + LN_EPS) * lng_ref[...] + lnb_ref[...]).astype(BF16)

  t_idx = lax.broadcasted_iota(jnp.int32, (SGU_CHUNK, SGU_CHUNK), 0)
  s_idx = lax.broadcasted_iota(jnp.int32, (SGU_CHUNK, SGU_CHUNK), 1)
  causal = s_idx <= t_idx
  for gi in range(SGU_GROUPS):
    w_g = jnp.where(causal, wsp_ref[0, gi], 0.0).astype(BF16)
    bias = jnp.broadcast_to(bsp_ref[:, gi:gi + 1], (SGU_CHUNK, group_dim))
    cols = slice(gi * group_dim, (gi + 1) * group_dim)
    for ci in range(rows // SGU_CHUNK):
      rws = slice(ci * SGU_CHUNK, (ci + 1) * SGU_CHUNK)
      mixed = _dot(w_g, v[rws, cols]) + bias
      gated_ref[rws, cols] = (u[rws, cols] * mixed).astype(BF16)
  o_ref[...] = x + _dot(gated_ref[...], wout_ref[0])


def _sgu(x, g, win, lng, lnb, wsp, bsp_t, wout, layer):
  t, d = x.shape
  rows = MLP_ROWS
  width = lng.shape[1]
  blocks = (_nbytes(win.shape[1:], BF16) + _nbytes(wout.shape[1:], BF16)
            + _nbytes(wsp.shape[1:], F32)
            + 4 * _nbytes((rows, d), F32) + _nbytes((rows, width), BF16))
  temps = 3 * _nbytes((rows, 2 * width), F32)
  return pl.pallas_call(
      _sgu_kernel,
      grid=(t // rows,),
      in_specs=[pl.BlockSpec((rows, d), lambda i: (i, 0)), _resident(g.shape),
                _layer_resident(win, layer), _resident(lng.shape), _resident(lnb.shape),
                _layer_resident(wsp, layer), _resident(bsp_t.shape),
                _layer_resident(wout, layer)],
      out_specs=pl.BlockSpec((rows, d), lambda i: (i, 0)),
      out_shape=jax.ShapeDtypeStruct((t, d), F32),
      scratch_shapes=[pltpu.VMEM((rows, width), BF16)],
      compiler_params=pltpu.CompilerParams(
          dimension_semantics=("arbitrary",),
          vmem_limit_bytes=_vmem_limit(blocks, temps)),
      name="sgu_mixer",
  )(x, g, win, lng, lnb, wsp, bsp_t, wout)


def _ffn_kernel(*refs, mla_out, final_norm):
  refs = list(refs)
  o_ref = refs.pop()
  x = refs.pop(0)[...]
  if mla_out:
    ol_ref, wuv_ref, wo_ref = refs[:3]
    del refs[:3]
    ol = ol_ref[...]
    heads = [_dot(ol[:, hd * KV_LORA_RANK:(hd + 1) * KV_LORA_RANK], wuv_ref[hd]).astype(BF16)
             for hd in range(MLA_HEADS)]
    x = x + _dot(jnp.concatenate(heads, axis=1), wo_ref[0])
  g_ref, wup_ref, wdn_ref = refs[:3]
  h = _rms(x, g_ref[...]).astype(BF16)
  a = jnp.square(jnp.maximum(_dot(h, wup_ref[0]), 0.0)).astype(BF16)
  y = x + _dot(a, wdn_ref[0])
  if final_norm:
    y = _rms(y, refs[3][...])
  o_ref[...] = y


def _ffn(x, g, wup, wdn, layer, mla_out=None, final_g=None):
  t, d = x.shape
  rows = MLP_ROWS
  hidden = wup.shape[2]
  row_spec = lambda width: pl.BlockSpec((rows, width), lambda i: (i, 0))
  args = [x]
  in_specs = [row_spec(d)]
  blocks = 4 * _nbytes((rows, d), F32)
  if mla_out is not None:
    o_lat, wuv, wo, wo_layer = mla_out
    args += [o_lat, wuv, wo]
    in_specs += [row_spec(o_lat.shape[1]), _resident(wuv.shape), _layer_resident(wo, wo_layer)]
    blocks += (2 * _nbytes((rows, o_lat.shape[1]), BF16) + _nbytes(wuv.shape, BF16)
               + _nbytes(wo.shape[1:], BF16))
  args += [g, wup, wdn]
  in_specs += [_resident(g.shape), _layer_resident(wup, layer), _layer_resident(wdn, layer)]
  blocks += _nbytes(wup.shape[1:], BF16) + _nbytes(wdn.shape[1:], BF16)
  if final_g is not None:
    args.append(final_g)
    in_specs.append(_resident(final_g.shape))
  temps = _nbytes((rows, hidden), F32) + _nbytes((rows, hidden), BF16) + 2 * _nbytes((rows, d), F32)
  return pl.pallas_call(
      functools.partial(_ffn_kernel, mla_out=mla_out is not None, final_norm=final_g is not None),
      grid=(t // rows,),
      in_specs=in_specs,
      out_specs=row_spec(d),
      out_shape=jax.ShapeDtypeStruct((t, d), F32),
      compiler_params=pltpu.CompilerParams(
          dimension_semantics=("arbitrary",),
          vmem_limit_bytes=_vmem_limit(blocks, temps)),
      name="ffn",
  )(*args)


def kernel(x, positions, norm_mix, norm_ffn, final_norm, mla_w_dkv, mla_q_norm, mla_kv_norm, mla_w_uq, mla_w_ukv, mla_w_o, sgu_w_in, sgu_ln_g, sgu_ln_b, sgu_w_spatial, sgu_b_spatial, sgu_w_out, ffn_w_up, ffn_w_down):
  batch, seq, d = x.shape
  depth = norm_mix.shape[0]
  t = batch * seq
  assert seq % MLP_ROWS == 0 and seq % ATTN_Q == 0 and t % ROPE_TABLE_COLS == 0
  xf = x.reshape(t, d)

  inv_freq = ROPE_THETA ** (-jnp.arange(0, QK_ROPE_DIM, 2, dtype=F32) / QK_ROPE_DIM)
  cos_t, sin_t = _rope_tables(positions.astype(F32).reshape(1, t), inv_freq.reshape(ROPE_HALF, 1))

  ffn_up, ffn_dn = ffn_w_up.astype(BF16), ffn_w_down.astype(BF16)
  sgu_in, sgu_out = sgu_w_in.astype(BF16), sgu_w_out.astype(BF16)
  mla_wo = mla_w_o.astype(BF16)

  row = lambda a: a.reshape(1, -1)
  for i in range(depth):
    j = i // 2
    mla_out = None
    if i % 2 == 0:
      wdkv = jnp.pad(mla_w_dkv[j], ((0, 0), (0, V7X_LANES - QK_ROPE_DIM))).astype(BF16)
      wuq = mla_w_uq[j].reshape(Q_LORA_RANK, MLA_HEADS, QK_HEAD_DIM)
      wqnt = wuq[:, :, :QK_NOPE_DIM].reshape(Q_LORA_RANK, -1).T.astype(BF16)
      wqrt = wuq[:, :, QK_NOPE_DIM:].reshape(Q_LORA_RANK, -1).T.astype(BF16)
      wukv = mla_w_ukv[j].reshape(KV_LORA_RANK, MLA_HEADS, QK_NOPE_DIM + V_HEAD_DIM)
      wuk = wukv[:, :, :QK_NOPE_DIM].transpose(1, 0, 2).astype(BF16)
      wuv = wukv[:, :, QK_NOPE_DIM:].transpose(1, 0, 2).astype(BF16)
      q_t, kc, ct = _mla_proj(xf, row(norm_mix[i]), wdkv, row(mla_q_norm[j]),
                              row(mla_kv_norm[j]), wqnt, wqrt, wuk, cos_t, sin_t)
      o_lat = _mla_attn(q_t, kc, ct, batch, seq)
      mla_out = (o_lat.reshape(t, -1), wuv, mla_wo, j)
    else:
      xf = _sgu(xf, row(norm_mix[i]), sgu_in, row(sgu_ln_g[j]), row(sgu_ln_b[j]),
                sgu_w_spatial, sgu_b_spatial[j].T, sgu_out, j)
    last = i == depth - 1
    xf = _ffn(xf, row(norm_ffn[i]), ffn_up, ffn_dn, i,
              mla_out=mla_out, final_g=row(final_norm) if last else None)
  return xf.reshape(batch, seq, d)
```

```python
import functools
import math

import jax
import jax.numpy as jnp
from jax import lax
from jax.experimental import pallas as pl
from jax.experimental.pallas import tpu as pltpu

F32 = jnp.float32
BF16 = jnp.bfloat16

MLA_HEADS = 8
QK_NOPE_DIM = 128
QK_ROPE_DIM = 64
QK_HEAD_DIM = QK_NOPE_DIM + QK_ROPE_DIM
V_HEAD_DIM = 128
Q_LORA_RANK = 256
KV_LORA_RANK = 128
ROPE_THETA = 10000.0
SGU_CHUNK = 128
SGU_GROUPS = 8
NORM_EPS = 1e-6
LN_EPS = 1e-5

V7X_LANES = 128
V7X_BF16_SUBLANES = 16
V7X_VMEM_BYTES = 64 * 1024 * 1024
V7X_VMEM_RESERVE_BYTES = 8 * 1024 * 1024

QK_PAD = KV_LORA_RANK + V7X_LANES
ROPE_HALF = QK_ROPE_DIM // 2
V_ROWS = KV_LORA_RANK + V7X_BF16_SUBLANES

ROPE_TABLE_COLS = 2048
ATTN_Q = 512
ATTN_K = 256
MLP_ROWS = 512
SGU_SUB_ROWS = 256
CAST_STEPS = 8


def _vmem_limit(block_bytes, temp_bytes):
  want = block_bytes + temp_bytes + V7X_VMEM_RESERVE_BYTES
  return int(min(want, V7X_VMEM_BYTES - V7X_VMEM_RESERVE_BYTES))


def _nbytes(shape, dtype):
  return math.prod(shape) * jnp.dtype(dtype).itemsize


def _dot(a, b):
  return jnp.dot(a, b, preferred_element_type=F32)


def _rms(x, g):
  return x * lax.rsqrt(jnp.mean(x * x, axis=-1, keepdims=True) + NORM_EPS) * g


def _resident(shape):
  nd = len(shape)
  return pl.BlockSpec(shape, lambda *_: (0,) * nd, pipeline_mode=pl.Buffered(1))


def _layer_resident(stacked, layer):
  nd = stacked.ndim
  return pl.BlockSpec((1,) + stacked.shape[1:], lambda *_: (layer,) + (0,) * (nd - 1),
                      pipeline_mode=pl.Buffered(1))


def _rope_table_kernel(pos_ref, freq_ref, cos_ref, sin_ref):
  ang = freq_ref[...] * pos_ref[...]
  cos_ref[...] = jnp.cos(ang)
  sin_ref[...] = jnp.sin(ang)


def _rope_tables(pos_row, freq_col):
  t = pos_row.shape[1]
  cols = ROPE_TABLE_COLS
  out = jax.ShapeDtypeStruct((ROPE_HALF, t), F32)
  return pl.pallas_call(
      _rope_table_kernel,
      grid=(t // cols,),
      in_specs=[pl.BlockSpec((1, cols), lambda i: (0, i)),
                pl.BlockSpec((ROPE_HALF, 1), lambda i: (0, 0))],
      out_specs=[pl.BlockSpec((ROPE_HALF, cols), lambda i: (0, i))] * 2,
      out_shape=[out, out],
      name="rope_tables",
  )(pos_row, freq_col)


def _rope_t(x_t, cos_t, sin_t):
  x1, x2 = x_t[:ROPE_HALF], x_t[ROPE_HALF:]
  return jnp.concatenate([x1 * cos_t - x2 * sin_t, x2 * cos_t + x1 * sin_t], axis=0)


def _mla_proj_kernel(x_ref, g_ref, wdkv_ref, qn_ref, kvn_ref, wqnt_ref, wqrt_ref,
                     wuk_ref, cos_ref, sin_ref, qt_ref, kc_ref, ct_ref):
  rows = x_ref.shape[0]
  h = _rms(x_ref[...], g_ref[...]).astype(BF16)
  lat = _dot(h, wdkv_ref[...])
  c_q = _rms(lat[:, :Q_LORA_RANK], qn_ref[...])
  c_kv = _rms(lat[:, Q_LORA_RANK:Q_LORA_RANK + KV_LORA_RANK], kvn_ref[...])
  cos_t = cos_ref[...]
  sin_t = sin_ref[...]
  pad = jnp.zeros((V7X_LANES - QK_ROPE_DIM, rows), F32)

  kr_t = lat[:, Q_LORA_RANK + KV_LORA_RANK:].T
  kr_roped = jnp.concatenate([_rope_t(kr_t[:QK_ROPE_DIM], cos_t, sin_t), pad], axis=0).T
  kc_ref[...] = jnp.concatenate([c_kv.astype(BF16), kr_roped.astype(BF16)], axis=1)
  ct_ref[0] = jnp.concatenate(
      [c_kv.T, jnp.ones((V_ROWS - KV_LORA_RANK, rows), F32)], axis=0).astype(BF16)

  scale = QK_HEAD_DIM ** -0.5 * math.log2(math.e)
  cq_t = c_q.T.astype(BF16)
  qn_t = _dot(wqnt_ref[...], cq_t)
  qr_t = _dot(wqrt_ref[...], cq_t)
  for hd in range(MLA_HEADS):
    qn_h = qn_t[hd * QK_NOPE_DIM:(hd + 1) * QK_NOPE_DIM].astype(BF16)
    q_lat = _dot(wuk_ref[hd], qn_h) * scale
    q_rope = _rope_t(qr_t[hd * QK_ROPE_DIM:(hd + 1) * QK_ROPE_DIM], cos_t, sin_t) * scale
    qt_ref[0, :, hd * rows:(hd + 1) * rows] = jnp.concatenate(
        [q_lat, q_rope, pad], axis=0).astype(BF16)


def _mla_proj(x, g, wdkv, qn, kvn, wqnt, wqrt, wuk, cos_t, sin_t):
  t, d = x.shape
  rows = ATTN_Q
  steps = t // rows
  row_spec = lambda width: pl.BlockSpec((rows, width), lambda i: (i, 0))
  table_spec = pl.BlockSpec((ROPE_HALF, rows), lambda i: (0, i))
  in_specs = [row_spec(d), _resident(g.shape), _resident(wdkv.shape), _resident(qn.shape),
              _resident(kvn.shape), _resident(wqnt.shape), _resident(wqrt.shape),
              _resident(wuk.shape), table_spec, table_spec]
  out_shape = [
      jax.ShapeDtypeStruct((steps, QK_PAD, MLA_HEADS * rows), BF16),
      jax.ShapeDtypeStruct((t, QK_PAD), BF16),
      jax.ShapeDtypeStruct((steps, V_ROWS, rows), BF16),
  ]
  out_specs = [
      pl.BlockSpec((1, QK_PAD, MLA_HEADS * rows), lambda i: (i, 0, 0)),
      row_spec(QK_PAD),
      pl.BlockSpec((1, V_ROWS, rows), lambda i: (i, 0, 0)),
  ]
  weights = sum(_nbytes(w.shape, w.dtype) for w in (wdkv, wqnt, wqrt, wuk))
  blocks = weights + 2 * (_nbytes((rows, d), F32) + 2 * _nbytes((ROPE_HALF, rows), F32)
                          + _nbytes((MLA_HEADS + 2, rows, QK_PAD), BF16))
  temps = 6 * _nbytes((rows, MLA_HEADS * QK_HEAD_DIM), F32)
  return pl.pallas_call(
      _mla_proj_kernel,
      grid=(steps,),
      in_specs=in_specs,
      out_specs=out_specs,
      out_shape=out_shape,
      compiler_params=pltpu.CompilerParams(
          dimension_semantics=("arbitrary",),
          vmem_limit_bytes=_vmem_limit(blocks, temps)),
      name="mla_proj",
  )(x, g, wdkv, qn, kvn, wqnt, wqrt, wuk, cos_t, sin_t)


_FULL, _DIAG, _SKIP = "full", "diag", "skip"


def _mla_attn_kernel(qt_ref, kc_ref, ct_ref, o_ref, s0_ref, s1_ref, mx0_ref, mx1_ref,
                     m_ref, acc_ref):
  qi = pl.program_id(1)
  tk = ATTN_K
  tq = qt_ref.shape[3] // MLA_HEADS
  units = qt_ref.shape[3] // tk
  bufs = ((s0_ref, mx0_ref), (s1_ref, mx1_ref))

  m_ref[...] = jnp.full(m_ref.shape, -jnp.inf, F32)
  acc_ref[...] = jnp.zeros(acc_ref.shape, F32)

  def keys(kt):
    return kc_ref[0, pl.ds(pl.multiple_of(kt * tk, tk), tk), :]

  def score_unit(k, buf, u):
    s_ref, mx_ref = bufs[buf]
    cols = slice(u * tk, (u + 1) * tk)
    s = _dot(k, qt_ref[0, 0, :, cols])
    s_ref[:, cols] = s
    mx_ref[:, cols] = jnp.max(s, axis=0, keepdims=True)

  def softmax_unit(v_t, buf, u, mode):
    s_ref, mx_ref = bufs[buf]
    cols = slice(u * tk, (u + 1) * tk)
    s = s_ref[:, cols]
    if mode == _DIAG:
      key = lax.broadcasted_iota(jnp.int32, s.shape, 0)
      qry = lax.broadcasted_iota(jnp.int32, s.shape, 1)
      s = jnp.where(key <= qry, s, -jnp.inf)
      mx = jnp.max(s, axis=0, keepdims=True)
    else:
      mx = mx_ref[:, cols]
    m_prev = m_ref[:, cols]
    m_new = jnp.maximum(m_prev, mx)
    alpha = jnp.exp2(m_prev - m_new)
    p = jnp.exp2(s - m_new).astype(BF16)
    acc_ref[:, cols] = alpha * acc_ref[:, cols] + _dot(v_t, p)
    m_ref[:, cols] = m_new

  def step(next_kt, next_units, sb, half, buf, mode_of_unit):
    k = None if next_kt is None else keys(next_kt)
    v_t = None if sb is None else ct_ref[0, sb, :, half * tk:(half + 1) * tk]
    for u in range(units):
      if k is not None and u in next_units:
        score_unit(k, 1 - buf, u)
      if sb is not None and mode_of_unit(u) != _SKIP:
        softmax_unit(v_t, buf, u, mode_of_unit(u))

  all_units = range(units)
  step(0, all_units, None, None, 1, None)

  def body(j, carry):
    step(2 * j + 1, all_units, j, 0, 0, lambda u: _FULL)
    step(2 * j + 2, all_units, j, 1, 1, lambda u: _FULL)
    return carry

  lax.fori_loop(0, qi, body, 0)
  step(2 * qi + 1, range(1, units, 2), qi, 0, 0, lambda u: _DIAG if u % 2 == 0 else _FULL)
  step(None, (), qi, 1, 1, lambda u: _SKIP if u % 2 == 0 else _DIAG)

  o_t = acc_ref[:KV_LORA_RANK, :] / acc_ref[KV_LORA_RANK:KV_LORA_RANK + 1, :]
  for hd in range(MLA_HEADS):
    o_ref[0, :, hd * KV_LORA_RANK:(hd + 1) * KV_LORA_RANK] = (
        o_t[:, hd * tq:(hd + 1) * tq].T.astype(o_ref.dtype))


def _mla_attn(q_t, kc, ct, batch, seq):
  tq, tk = ATTN_Q, ATTN_K
  assert tq == 2 * tk
  nq = MLA_HEADS * tq
  q_tiles = seq // tq
  qt4 = q_t.reshape(batch, q_tiles, QK_PAD, nq)
  kc3 = kc.reshape(batch, seq, QK_PAD)
  ct4 = ct.reshape(batch, q_tiles, V_ROWS, tq)
  out_w = MLA_HEADS * KV_LORA_RANK
  blocks = 2 * (_nbytes((QK_PAD, nq), BF16) + _nbytes((seq, QK_PAD), BF16)
                + _nbytes((V_ROWS, seq), BF16) + _nbytes((tq, out_w), BF16))
  scratch = _nbytes((2 * tk + V_ROWS + 24, nq), F32)
  temps = 2 * _nbytes((tk, nq), F32)
  return pl.pallas_call(
      _mla_attn_kernel,
      grid=(batch, q_tiles),
      in_specs=[
          pl.BlockSpec((1, 1, QK_PAD, nq), lambda b, i: (b, i, 0, 0)),
          pl.BlockSpec((1, seq, QK_PAD), lambda b, i: (b, 0, 0)),
          pl.BlockSpec((1, q_tiles, V_ROWS, tq), lambda b, i: (b, 0, 0, 0)),
      ],
      out_specs=pl.BlockSpec((1, tq, out_w), lambda b, i: (b, i, 0)),
      out_shape=jax.ShapeDtypeStruct((batch, seq, out_w), BF16),
      scratch_shapes=[pltpu.VMEM((tk, nq), F32), pltpu.VMEM((tk, nq), F32),
                      pltpu.VMEM((1, nq), F32), pltpu.VMEM((1, nq), F32),
                      pltpu.VMEM((1, nq), F32), pltpu.VMEM((V_ROWS, nq), F32)],
      compiler_params=pltpu.CompilerParams(
          dimension_semantics=("arbitrary", "arbitrary"),
          vmem_limit_bytes=_vmem_limit(blocks + scratch, temps)),
      name="mla_attn",
  )(qt4, kc3, ct4)


def _cast_chunk_spec(stacked, layer):
  _, r, c = stacked.shape
  return pl.BlockSpec((1, r // CAST_STEPS, c),
                      lambda s: (layer, jnp.minimum(s, CAST_STEPS - 1), 0))


def _cast_chunk(step, src_ref, dst_ref):
  chunk = src_ref.shape[1]
  dst_ref[pl.ds(pl.multiple_of(step * chunk, chunk), chunk), :] = src_ref[0].astype(BF16)


def _tile_spec(rows, width):
  return pl.BlockSpec((rows, width), lambda s: (jnp.maximum(s - CAST_STEPS, 0), 0))


def _sgu_kernel(x_ref, g_ref, win_ref, lng_ref, lnb_ref, wsp_ref, bsp_ref, wout_ref,
                o_ref, win_bf, wout_bf, gated_ref):
  step = pl.program_id(0)

  @pl.when(step < CAST_STEPS)
  def _():
    _cast_chunk(step, win_ref, win_bf)
    _cast_chunk(step, wout_ref, wout_bf)

  @pl.when(step >= CAST_STEPS)
  def _():
    rows = x_ref.shape[0]
    width = lng_ref.shape[1]
    group_dim = width // SGU_GROUPS
    t_idx = lax.broadcasted_iota(jnp.int32, (SGU_CHUNK, SGU_CHUNK), 0)
    s_idx = lax.broadcasted_iota(jnp.int32, (SGU_CHUNK, SGU_CHUNK), 1)
    causal = s_idx <= t_idx
    w_sp = [jnp.where(causal, wsp_ref[0, gi], 0.0).astype(BF16)
            for gi in range(SGU_GROUPS)]

    subs = [slice(r0, r0 + SGU_SUB_ROWS) for r0 in range(0, rows, SGU_SUB_ROWS)]
    zs = [_dot(_rms(x_ref[sub, :], g_ref[...]).astype(BF16), win_bf[...]) for sub in subs]
    for sub, z in zip(subs, zs):
      r0 = sub.start
      z = 0.5 * z * (1.0 + lax.erf(z * (0.5 ** 0.5)))
      u = z[:, :width]
      v = z[:, width:]
      mu = jnp.mean(v, axis=-1, keepdims=True)
      var = jnp.mean(jnp.square(v - mu), axis=-1, keepdims=True)
      v = ((v - mu) * lax.rsqrt(var + LN_EPS) * lng_ref[...] + lnb_ref[...]).astype(BF16)
      for gi in range(SGU_GROUPS):
        bias = jnp.broadcast_to(bsp_ref[:, gi:gi + 1], (SGU_CHUNK, group_dim))
        cols = slice(gi * group_dim, (gi + 1) * group_dim)
        for c0 in range(0, SGU_SUB_ROWS, SGU_CHUNK):
          rws = slice(c0, c0 + SGU_CHUNK)
          mixed = _dot(w_sp[gi], v[rws, cols]) + bias
          gated_ref[r0 + c0:r0 + c0 + SGU_CHUNK, cols] = (u[rws, cols] * mixed).astype(BF16)
      o_ref[sub, :] = x_ref[sub, :] + _dot(gated_ref[sub, :], wout_bf[...])


def _sgu(x, g, win, lng, lnb, wsp, bsp_t, wout, layer):
  t, d = x.shape
  rows = MLP_ROWS
  width = lng.shape[1]
  scratch = [pltpu.VMEM(win.shape[1:], BF16), pltpu.VMEM(wout.shape[1:], BF16),
             pltpu.VMEM((rows, width), BF16)]
  blocks = (2 * (_nbytes(win.shape[1:], F32) + _nbytes(wout.shape[1:], F32)) // CAST_STEPS
            + _nbytes(win.shape[1:], BF16) + _nbytes(wout.shape[1:], BF16)
            + _nbytes(wsp.shape[1:], F32)
            + 4 * _nbytes((rows, d), F32) + _nbytes((rows, width), BF16))
  temps = 3 * _nbytes((rows, 2 * width), F32)
  return pl.pallas_call(
      _sgu_kernel,
      grid=(CAST_STEPS + t // rows,),
      in_specs=[_tile_spec(rows, d), _resident(g.shape),
                _cast_chunk_spec(win, layer), _resident(lng.shape), _resident(lnb.shape),
                _layer_resident(wsp, layer), _resident(bsp_t.shape),
                _cast_chunk_spec(wout, layer)],
      out_specs=_tile_spec(rows, d),
      out_shape=jax.ShapeDtypeStruct((t, d), F32),
      scratch_shapes=scratch,
      compiler_params=pltpu.CompilerParams(
          dimension_semantics=("arbitrary",),
          vmem_limit_bytes=_vmem_limit(blocks, temps)),
      name="sgu_mixer",
  )(x, g, win, lng, lnb, wsp, bsp_t, wout)


def _ffn_kernel(*refs, mla_out, final_norm):
  refs = list(refs)
  x_ref = refs.pop(0)
  if mla_out:
    ol_ref, wuv_ref, wo_ref = refs[:3]
    del refs[:3]
  g_ref, wup_ref, wdn_ref = refs[:3]
  del refs[:3]
  gf_ref = refs.pop(0) if final_norm else None
  o_ref = refs.pop(0)
  wo_bf = refs.pop(0) if mla_out else None
  wup_bf, wdn_bf = refs
  step = pl.program_id(0)

  @pl.when(step < CAST_STEPS)
  def _():
    _cast_chunk(step, wup_ref, wup_bf)
    _cast_chunk(step, wdn_ref, wdn_bf)
    if mla_out:
      _cast_chunk(step, wo_ref, wo_bf)

  @pl.when(step >= CAST_STEPS)
  def _():
    x = x_ref[...]
    if mla_out:
      ol = ol_ref[...]
      heads = [_dot(ol[:, hd * KV_LORA_RANK:(hd + 1) * KV_LORA_RANK], wuv_ref[hd]).astype(BF16)
               for hd in range(MLA_HEADS)]
      x = x + _dot(jnp.concatenate(heads, axis=1), wo_bf[...])
    h = _rms(x, g_ref[...]).astype(BF16)
    a = jnp.square(jnp.maximum(_dot(h, wup_bf[...]), 0.0)).astype(BF16)
    y = x + _dot(a, wdn_bf[...])
    if final_norm:
      y = _rms(y, gf_ref[...])
    o_ref[...] = y


def _ffn(x, g, wup, wdn, layer, mla_out=None, final_g=None):
  t, d = x.shape
  rows = MLP_ROWS
  hidden = wup.shape[2]
  args = [x]
  in_specs = [_tile_spec(rows, d)]
  scratch = []
  blocks = 4 * _nbytes((rows, d), F32)
  chunked = [wup, wdn]
  if mla_out is not None:
    o_lat, wuv, wo, wo_layer = mla_out
    args += [o_lat, wuv, wo]
    in_specs += [_tile_spec(rows, o_lat.shape[1]), _resident(wuv.shape),
                 _cast_chunk_spec(wo, wo_layer)]
    scratch.append(pltpu.VMEM(wo.shape[1:], BF16))
    blocks += 2 * _nbytes((rows, o_lat.shape[1]), BF16) + _nbytes(wuv.shape, BF16)
    chunked.append(wo)
  args += [g, wup, wdn]
  in_specs += [_resident(g.shape), _cast_chunk_spec(wup, layer), _cast_chunk_spec(wdn, layer)]
  scratch += [pltpu.VMEM(wup.shape[1:], BF16), pltpu.VMEM(wdn.shape[1:], BF16)]
  blocks += sum(_nbytes(w.shape[1:], BF16) + 2 * _nbytes(w.shape[1:], F32) // CAST_STEPS
                for w in chunked)
  if final_g is not None:
    args.append(final_g)
    in_specs.append(_resident(final_g.shape))
  temps = _nbytes((rows, hidden), F32) + _nbytes((rows, hidden), BF16) + 2 * _nbytes((rows, d), F32)
  return pl.pallas_call(
      functools.partial(_ffn_kernel, mla_out=mla_out is not None, final_norm=final_g is not None),
      grid=(CAST_STEPS + t // rows,),
      in_specs=in_specs,
      out_specs=_tile_spec(rows, d),
      out_shape=jax.ShapeDtypeStruct((t, d), F32),
      scratch_shapes=scratch,
      compiler_params=pltpu.CompilerParams(
          dimension_semantics=("arbitrary",),
          vmem_limit_bytes=_vmem_limit(blocks, temps)),
      name="ffn",
  )(*args)


def kernel(x, positions, norm_mix, norm_ffn, final_norm, mla_w_dkv, mla_q_norm, mla_kv_norm, mla_w_uq, mla_w_ukv, mla_w_o, sgu_w_in, sgu_ln_g, sgu_ln_b, sgu_w_spatial, sgu_b_spatial, sgu_w_out, ffn_w_up, ffn_w_down):
  batch, seq, d = x.shape
  depth = norm_mix.shape[0]
  t = batch * seq
  assert seq % MLP_ROWS == 0 and seq % ATTN_Q == 0 and t % ROPE_TABLE_COLS == 0
  xf = x.reshape(t, d)

  inv_freq = ROPE_THETA ** (-jnp.arange(0, QK_ROPE_DIM, 2, dtype=F32) / QK_ROPE_DIM)
  cos_t, sin_t = _rope_tables(positions.astype(F32).reshape(1, t), inv_freq.reshape(ROPE_HALF, 1))

  row = lambda a: a.reshape(1, -1)
  for i in range(depth):
    j = i // 2
    mla_out = None
    if i % 2 == 0:
      wdkv = jnp.pad(mla_w_dkv[j], ((0, 0), (0, V7X_LANES - QK_ROPE_DIM))).astype(BF16)
      wuq = mla_w_uq[j].reshape(Q_LORA_RANK, MLA_HEADS, QK_HEAD_DIM)
      wqnt = wuq[:, :, :QK_NOPE_DIM].reshape(Q_LORA_RANK, -1).T.astype(BF16)
      wqrt = wuq[:, :, QK_NOPE_DIM:].reshape(Q_LORA_RANK, -1).T.astype(BF16)
      wukv = mla_w_ukv[j].reshape(KV_LORA_RANK, MLA_HEADS, QK_NOPE_DIM + V_HEAD_DIM)
      wuk = wukv[:, :, :QK_NOPE_DIM].transpose(1, 0, 2).astype(BF16)
      wuv = wukv[:, :, QK_NOPE_DIM:].transpose(1, 0, 2).astype(BF16)
      q_t, kc, ct = _mla_proj(xf, row(norm_mix[i]), wdkv, row(mla_q_norm[j]),
                              row(mla_kv_norm[j]), wqnt, wqrt, wuk, cos_t, sin_t)
      o_lat = _mla_attn(q_t, kc, ct, batch, seq)
      mla_out = (o_lat.reshape(t, -1), wuv, mla_w_o, j)
    else:
      xf = _sgu(xf, row(norm_mix[i]), sgu_w_in, row(sgu_ln_g[j]), row(sgu_ln_b[j]),
                sgu_w_spatial, sgu_b_spatial[j].T, sgu_w_out, j)
    last = i == depth - 1
    xf = _ffn(xf, row(norm_ffn[i]), ffn_w_up, ffn_w_down, i,
              mla_out=mla_out, final_g=row(final_norm) if last else None)
  return xf.reshape(batch, seq, d)
```

```python
import functools
import math

import jax
import jax.numpy as jnp
from jax import lax
from jax.experimental import pallas as pl
from jax.experimental.pallas import tpu as pltpu

F32 = jnp.float32
BF16 = jnp.bfloat16

MLA_HEADS = 8
QK_NOPE_DIM = 128
QK_ROPE_DIM = 64
QK_HEAD_DIM = QK_NOPE_DIM + QK_ROPE_DIM
V_HEAD_DIM = 128
Q_LORA_RANK = 256
KV_LORA_RANK = 128
ROPE_THETA = 10000.0
SGU_CHUNK = 128
SGU_GROUPS = 8
NORM_EPS = 1e-6
LN_EPS = 1e-5

V7X_LANES = 128
V7X_BF16_SUBLANES = 16
V7X_VMEM_BYTES = 64 * 1024 * 1024
V7X_VMEM_RESERVE_BYTES = 8 * 1024 * 1024

QK_PAD = KV_LORA_RANK + V7X_LANES
ROPE_HALF = QK_ROPE_DIM // 2
V_ROWS = KV_LORA_RANK + V7X_BF16_SUBLANES

ROPE_TABLE_COLS = 2048
ATTN_Q = 512
ATTN_K = 256
MLP_ROWS = 512
SGU_SUB_ROWS = 256
CAST_STEPS = 8


def _vmem_limit(block_bytes, temp_bytes):
  want = block_bytes + temp_bytes + V7X_VMEM_RESERVE_BYTES
  return int(min(want, V7X_VMEM_BYTES - V7X_VMEM_RESERVE_BYTES))


def _nbytes(shape, dtype):
  return math.prod(shape) * jnp.dtype(dtype).itemsize


def _dot(a, b):
  return jnp.dot(a, b, preferred_element_type=F32)


def _rms(x, g):
  return x * lax.rsqrt(jnp.mean(x * x, axis=-1, keepdims=True) + NORM_EPS) * g


def _resident(shape):
  nd = len(shape)
  return pl.BlockSpec(shape, lambda *_: (0,) * nd, pipeline_mode=pl.Buffered(1))


def _layer_resident(stacked, layer):
  nd = stacked.ndim
  return pl.BlockSpec((1,) + stacked.shape[1:], lambda *_: (layer,) + (0,) * (nd - 1),
                      pipeline_mode=pl.Buffered(1))


def _rope_table_kernel(pos_ref, freq_ref, cos_ref, sin_ref):
  ang = freq_ref[...] * pos_ref[...]
  cos_ref[...] = jnp.cos(ang)
  sin_ref[...] = jnp.sin(ang)


def _rope_tables(pos_row, freq_col):
  t = pos_row.shape[1]
  cols = ROPE_TABLE_COLS
  out = jax.ShapeDtypeStruct((ROPE_HALF, t), F32)
  return pl.pallas_call(
      _rope_table_kernel,
      grid=(t // cols,),
      in_specs=[pl.BlockSpec((1, cols), lambda i: (0, i)),
                pl.BlockSpec((ROPE_HALF, 1), lambda i: (0, 0))],
      out_specs=[pl.BlockSpec((ROPE_HALF, cols), lambda i: (0, i))] * 2,
      out_shape=[out, out],
      name="rope_tables",
  )(pos_row, freq_col)


def _rope_t(x_t, cos_t, sin_t):
  x1, x2 = x_t[:ROPE_HALF], x_t[ROPE_HALF:]
  return jnp.concatenate([x1 * cos_t - x2 * sin_t, x2 * cos_t + x1 * sin_t], axis=0)


def _mla_proj_kernel(x_ref, g_ref, wdkv_ref, qn_ref, kvn_ref, wqnt_ref, wqrt_ref,
                     wuk_ref, cos_ref, sin_ref, qt_ref, kc_ref, ct_ref, wql_ref):
  rows = x_ref.shape[0]
  scale = QK_HEAD_DIM ** -0.5 * math.log2(math.e)

  @pl.when(pl.program_id(0) == 0)
  def _():
    for hd in range(MLA_HEADS):
      rws = slice(hd * QK_NOPE_DIM, (hd + 1) * QK_NOPE_DIM)
      wql_ref[rws, :] = (_dot(wuk_ref[hd], wqnt_ref[rws, :]) * scale).astype(BF16)

  h = _rms(x_ref[...], g_ref[...]).astype(BF16)
  lat = _dot(h, wdkv_ref[...])
  c_q = _rms(lat[:, :Q_LORA_RANK], qn_ref[...])
  c_kv = _rms(lat[:, Q_LORA_RANK:Q_LORA_RANK + KV_LORA_RANK], kvn_ref[...])
  cos_t = cos_ref[...]
  sin_t = sin_ref[...]
  pad = jnp.zeros((V7X_LANES - QK_ROPE_DIM, rows), F32)

  kr_t = lat[:, Q_LORA_RANK + KV_LORA_RANK:].T
  kr_roped = jnp.concatenate([_rope_t(kr_t[:QK_ROPE_DIM], cos_t, sin_t), pad], axis=0).T
  kc_ref[...] = jnp.concatenate([c_kv.astype(BF16), kr_roped.astype(BF16)], axis=1)
  ct_ref[0] = jnp.concatenate(
      [c_kv.T, jnp.ones((V_ROWS - KV_LORA_RANK, rows), F32)], axis=0).astype(BF16)

  cq_t = c_q.T.astype(BF16)
  ql_t = _dot(wql_ref[...], cq_t)
  qr_t = _dot(wqrt_ref[...], cq_t)
  for hd in range(MLA_HEADS):
    q_lat = ql_t[hd * KV_LORA_RANK:(hd + 1) * KV_LORA_RANK]
    q_rope = _rope_t(qr_t[hd * QK_ROPE_DIM:(hd + 1) * QK_ROPE_DIM], cos_t, sin_t) * scale
    qt_ref[0, :, hd * rows:(hd + 1) * rows] = jnp.concatenate(
        [q_lat, q_rope, pad], axis=0).astype(BF16)


def _mla_proj(x, g, wdkv, qn, kvn, wqnt, wqrt, wuk, cos_t, sin_t):
  t, d = x.shape
  rows = ATTN_Q
  steps = t // rows
  row_spec = lambda width: pl.BlockSpec((rows, width), lambda i: (i, 0))
  table_spec = pl.BlockSpec((ROPE_HALF, rows), lambda i: (0, i))
  in_specs = [row_spec(d), _resident(g.shape), _resident(wdkv.shape), _resident(qn.shape),
              _resident(kvn.shape), _resident(wqnt.shape), _resident(wqrt.shape),
              _resident(wuk.shape), table_spec, table_spec]
  out_shape = [
      jax.ShapeDtypeStruct((steps, QK_PAD, MLA_HEADS * rows), BF16),
      jax.ShapeDtypeStruct((t, QK_PAD), BF16),
      jax.ShapeDtypeStruct((steps, V_ROWS, rows), BF16),
  ]
  out_specs = [
      pl.BlockSpec((1, QK_PAD, MLA_HEADS * rows), lambda i: (i, 0, 0)),
      row_spec(QK_PAD),
      pl.BlockSpec((1, V_ROWS, rows), lambda i: (i, 0, 0)),
  ]
  weights = sum(_nbytes(w.shape, w.dtype) for w in (wdkv, wqnt, wqrt, wuk))
  blocks = weights + 2 * (_nbytes((rows, d), F32) + 2 * _nbytes((ROPE_HALF, rows), F32)
                          + _nbytes((MLA_HEADS + 2, rows, QK_PAD), BF16))
  temps = 6 * _nbytes((rows, MLA_HEADS * QK_HEAD_DIM), F32)
  return pl.pallas_call(
      _mla_proj_kernel,
      grid=(steps,),
      in_specs=in_specs,
      out_specs=out_specs,
      out_shape=out_shape,
      scratch_shapes=[pltpu.VMEM((MLA_HEADS * KV_LORA_RANK, Q_LORA_RANK), BF16)],
      compiler_params=pltpu.CompilerParams(
          dimension_semantics=("arbitrary",),
          vmem_limit_bytes=_vmem_limit(blocks, temps)),
      name="mla_proj",
  )(x, g, wdkv, qn, kvn, wqnt, wqrt, wuk, cos_t, sin_t)


_FULL, _DIAG, _SKIP = "full", "diag", "skip"


def _mla_attn_kernel(qt_ref, kc_ref, ct_ref, qn_ref, kn_ref, o_ref, s0_ref, s1_ref,
                     mx0_ref, mx1_ref, m_ref, acc_ref):
  qi = pl.program_id(1)
  tk = ATTN_K
  tq = qt_ref.shape[3] // MLA_HEADS
  units = qt_ref.shape[3] // tk
  bufs = ((s0_ref, mx0_ref), (s1_ref, mx1_ref))

  m_ref[...] = jnp.full(m_ref.shape, -jnp.inf, F32)
  acc_ref[...] = jnp.zeros(acc_ref.shape, F32)

  def keys(kt):
    return kc_ref[0, pl.ds(pl.multiple_of(kt * tk, tk), tk), :]

  def score_unit(k, q_ref, buf, u):
    s_ref, mx_ref = bufs[buf]
    cols = slice(u * tk, (u + 1) * tk)
    s = _dot(k, q_ref[0, 0, :, cols])
    s_ref[:, cols] = s
    mx_ref[:, cols] = jnp.max(s, axis=0, keepdims=True)

  def softmax_unit(v_t, buf, u, mode):
    s_ref, mx_ref = bufs[buf]
    cols = slice(u * tk, (u + 1) * tk)
    s = s_ref[:, cols]
    if mode == _DIAG:
      key = lax.broadcasted_iota(jnp.int32, s.shape, 0)
      qry = lax.broadcasted_iota(jnp.int32, s.shape, 1)
      s = jnp.where(key <= qry, s, -jnp.inf)
      mx = jnp.max(s, axis=0, keepdims=True)
    else:
      mx = mx_ref[:, cols]
    m_prev = m_ref[:, cols]
    m_new = jnp.maximum(m_prev, mx)
    alpha = jnp.exp2(m_prev - m_new)
    p = jnp.exp2(s - m_new).astype(BF16)
    acc_ref[:, cols] = alpha * acc_ref[:, cols] + _dot(v_t, p)
    m_ref[:, cols] = m_new

  def step(nxt, next_units, sb, half, buf, mode_of_unit):
    v_t = None if sb is None else ct_ref[0, sb, :, half * tk:(half + 1) * tk]
    for u in range(units):
      if u in next_units:
        score_unit(nxt[0], nxt[1], 1 - buf, u)
      if sb is not None and mode_of_unit(u) != _SKIP:
        softmax_unit(v_t, buf, u, mode_of_unit(u))

  all_units = range(units)

  @pl.when(jnp.logical_and(pl.program_id(0) == 0, qi == 0))
  def _():
    step((keys(0), qt_ref), all_units, None, None, 1, None)

  def body(j, carry):
    step((keys(2 * j + 1), qt_ref), all_units, j, 0, 0, lambda u: _FULL)
    step((keys(2 * j + 2), qt_ref), all_units, j, 1, 1, lambda u: _FULL)
    return carry

  lax.fori_loop(0, qi, body, 0)
  step((keys(2 * qi + 1), qt_ref), range(1, units, 2), qi, 0, 0,
       lambda u: _DIAG if u % 2 == 0 else _FULL)
  step((kn_ref[0], qn_ref), all_units, qi, 1, 1, lambda u: _SKIP if u % 2 == 0 else _DIAG)

  o_t = acc_ref[:KV_LORA_RANK, :] / acc_ref[KV_LORA_RANK:KV_LORA_RANK + 1, :]
  for hd in range(MLA_HEADS):
    o_ref[0, :, hd * KV_LORA_RANK:(hd + 1) * KV_LORA_RANK] = (
        o_t[:, hd * tq:(hd + 1) * tq].T.astype(o_ref.dtype))


def _mla_attn(q_t, kc, ct, batch, seq):
  tq, tk = ATTN_Q, ATTN_K
  assert tq == 2 * tk
  nq = MLA_HEADS * tq
  q_tiles = seq // tq
  qt4 = q_t.reshape(batch, q_tiles, QK_PAD, nq)
  kc3 = kc.reshape(batch, seq, QK_PAD)
  ct4 = ct.reshape(batch, q_tiles, V_ROWS, tq)
  out_w = MLA_HEADS * KV_LORA_RANK
  blocks = 2 * (2 * _nbytes((QK_PAD, nq), BF16) + _nbytes((seq + tk, QK_PAD), BF16)
                + _nbytes((V_ROWS, seq), BF16) + _nbytes((tq, out_w), BF16))

  def next_tile(b, i):
    flat = jnp.minimum(b * q_tiles + i + 1, batch * q_tiles - 1)
    return flat // q_tiles, flat % q_tiles

  scratch = _nbytes((2 * tk + V_ROWS + 24, nq), F32)
  temps = 2 * _nbytes((tk, nq), F32)
  return pl.pallas_call(
      _mla_attn_kernel,
      grid=(batch, q_tiles),
      in_specs=[
          pl.BlockSpec((1, 1, QK_PAD, nq), lambda b, i: (b, i, 0, 0)),
          pl.BlockSpec((1, seq, QK_PAD), lambda b, i: (b, 0, 0)),
          pl.BlockSpec((1, q_tiles, V_ROWS, tq), lambda b, i: (b, 0, 0, 0)),
          pl.BlockSpec((1, 1, QK_PAD, nq), lambda b, i: (*next_tile(b, i), 0, 0)),
          pl.BlockSpec((1, tk, QK_PAD), lambda b, i: (next_tile(b, i)[0], 0, 0)),
      ],
      out_specs=pl.BlockSpec((1, tq, out_w), lambda b, i: (b, i, 0)),
      out_shape=jax.ShapeDtypeStruct((batch, seq, out_w), BF16),
      scratch_shapes=[pltpu.VMEM((tk, nq), F32), pltpu.VMEM((tk, nq), F32),
                      pltpu.VMEM((1, nq), F32), pltpu.VMEM((1, nq), F32),
                      pltpu.VMEM((1, nq), F32), pltpu.VMEM((V_ROWS, nq), F32)],
      compiler_params=pltpu.CompilerParams(
          dimension_semantics=("arbitrary", "arbitrary"),
          vmem_limit_bytes=_vmem_limit(blocks + scratch, temps)),
      name="mla_attn",
  )(qt4, kc3, ct4, qt4, kc3)


def _cast_chunk_spec(stacked, layer):
  _, r, c = stacked.shape
  return pl.BlockSpec((1, r // CAST_STEPS, c),
                      lambda s: (layer, jnp.minimum(s, CAST_STEPS - 1), 0))


def _cast_chunk(step, src_ref, dst_ref):
  chunk = src_ref.shape[1]
  dst_ref[pl.ds(pl.multiple_of(step * chunk, chunk), chunk), :] = src_ref[0].astype(BF16)


def _tile_spec(rows, width):
  return pl.BlockSpec((rows, width), lambda s: (jnp.maximum(s - CAST_STEPS, 0), 0))


def _sgu_kernel(x_ref, g_ref, win_ref, lng_ref, lnb_ref, wsp_ref, bsp_ref, wout_ref,
                o_ref, win_bf, wout_bf, gated_ref):
  step = pl.program_id(0)

  @pl.when(step < CAST_STEPS)
  def _():
    _cast_chunk(step, win_ref, win_bf)
    _cast_chunk(step, wout_ref, wout_bf)

  @pl.when(step >= CAST_STEPS)
  def _():
    rows = x_ref.shape[0]
    width = lng_ref.shape[1]
    group_dim = width // SGU_GROUPS
    t_idx = lax.broadcasted_iota(jnp.int32, (SGU_CHUNK, SGU_CHUNK), 0)
    s_idx = lax.broadcasted_iota(jnp.int32, (SGU_CHUNK, SGU_CHUNK), 1)
    causal = s_idx <= t_idx
    w_sp = [jnp.where(causal, wsp_ref[0, gi], 0.0).astype(BF16)
            for gi in range(SGU_GROUPS)]

    subs = [slice(r0, r0 + SGU_SUB_ROWS) for r0 in range(0, rows, SGU_SUB_ROWS)]
    zs = [_dot(_rms(x_ref[sub, :], g_ref[...]).astype(BF16), win_bf[...]) for sub in subs]
    for sub, z in zip(subs, zs):
      r0 = sub.start
      z = 0.5 * z * (1.0 + lax.erf(z * (0.5 ** 0.5)))
      u = z[:, :width]
      v = z[:, width:]
      mu = jnp.mean(v, axis=-1, keepdims=True)
      var = jnp.mean(jnp.square(v - mu), axis=-1, keepdims=True)
      v = ((v - mu) * lax.rsqrt(var + LN_EPS) * lng_ref[...] + lnb_ref[...]).astype(BF16)
      for gi in range(SGU_GROUPS):
        bias = jnp.broadcast_to(bsp_ref[:, gi:gi + 1], (SGU_CHUNK, group_dim))
        cols = slice(gi * group_dim, (gi + 1) * group_dim)
        for c0 in range(0, SGU_SUB_ROWS, SGU_CHUNK):
          rws = slice(c0, c0 + SGU_CHUNK)
          mixed = _dot(w_sp[gi], v[rws, cols]) + bias
          gated_ref[r0 + c0:r0 + c0 + SGU_CHUNK, cols] = (u[rws, cols] * mixed).astype(BF16)
      o_ref[sub, :] = x_ref[sub, :] + _dot(gated_ref[sub, :], wout_bf[...])


def _sgu(x, g, win, lng, lnb, wsp, bsp_t, wout, layer):
  t, d = x.shape
  rows = MLP_ROWS
  width = lng.shape[1]
  scratch = [pltpu.VMEM(win.shape[1:], BF16), pltpu.VMEM(wout.shape[1:], BF16),
             pltpu.VMEM((rows, width), BF16)]
  blocks = (2 * (_nbytes(win.shape[1:], F32) + _nbytes(wout.shape[1:], F32)) // CAST_STEPS
            + _nbytes(win.shape[1:], BF16) + _nbytes(wout.shape[1:], BF16)
            + _nbytes(wsp.shape[1:], F32)
            + 4 * _nbytes((rows, d), F32) + _nbytes((rows, width), BF16))
  temps = 3 * _nbytes((rows, 2 * width), F32)
  return pl.pallas_call(
      _sgu_kernel,
      grid=(CAST_STEPS + t // rows,),
      in_specs=[_tile_spec(rows, d), _resident(g.shape),
                _cast_chunk_spec(win, layer), _resident(lng.shape), _resident(lnb.shape),
                _layer_resident(wsp, layer), _resident(bsp_t.shape),
                _cast_chunk_spec(wout, layer)],
      out_specs=_tile_spec(rows, d),
      out_shape=jax.ShapeDtypeStruct((t, d), F32),
      scratch_shapes=scratch,
      compiler_params=pltpu.CompilerParams(
          dimension_semantics=("arbitrary",),
          vmem_limit_bytes=_vmem_limit(blocks, temps)),
      name="sgu_mixer",
  )(x, g, win, lng, lnb, wsp, bsp_t, wout)


def _ffn_kernel(*refs, mla_out, final_norm):
  refs = list(refs)
  x_ref = refs.pop(0)
  if mla_out:
    ol_ref, wuv_ref, wo_ref = refs[:3]
    del refs[:3]
  g_ref, wup_ref, wdn_ref = refs[:3]
  del refs[:3]
  gf_ref = refs.pop(0) if final_norm else None
  o_ref = refs.pop(0)
  wo_bf = refs.pop(0) if mla_out else None
  wup_bf, wdn_bf = refs
  step = pl.program_id(0)

  @pl.when(step < CAST_STEPS)
  def _():
    _cast_chunk(step, wup_ref, wup_bf)
    _cast_chunk(step, wdn_ref, wdn_bf)
    if mla_out:
      rws = pl.ds(pl.multiple_of(step * V_HEAD_DIM, V_HEAD_DIM), V_HEAD_DIM)
      wo_bf[rws, :] = _dot(wuv_ref[step], wo_ref[0].astype(BF16)).astype(BF16)

  @pl.when(step >= CAST_STEPS)
  def _():
    x = x_ref[...]
    if mla_out:
      x = x + _dot(ol_ref[...], wo_bf[...])
    h = _rms(x, g_ref[...]).astype(BF16)
    a = jnp.square(jnp.maximum(_dot(h, wup_bf[...]), 0.0)).astype(BF16)
    y = x + _dot(a, wdn_bf[...])
    if final_norm:
      y = _rms(y, gf_ref[...])
    o_ref[...] = y


def _ffn(x, g, wup, wdn, layer, mla_out=None, final_g=None):
  t, d = x.shape
  rows = MLP_ROWS
  hidden = wup.shape[2]
  args = [x]
  in_specs = [_tile_spec(rows, d)]
  scratch = []
  blocks = 4 * _nbytes((rows, d), F32)
  chunked = [wup, wdn]
  if mla_out is not None:
    o_lat, wuv, wo, wo_layer = mla_out
    assert CAST_STEPS == MLA_HEADS and wo.shape[1] == MLA_HEADS * V_HEAD_DIM
    args += [o_lat, wuv, wo]
    in_specs += [_tile_spec(rows, o_lat.shape[1]), _resident(wuv.shape),
                 _cast_chunk_spec(wo, wo_layer)]
    scratch.append(pltpu.VMEM(wo.shape[1:], BF16))
    blocks += 2 * _nbytes((rows, o_lat.shape[1]), BF16) + _nbytes(wuv.shape, BF16)
    chunked.append(wo)
  args += [g, wup, wdn]
  in_specs += [_resident(g.shape), _cast_chunk_spec(wup, layer), _cast_chunk_spec(wdn, layer)]
  scratch += [pltpu.VMEM(wup.shape[1:], BF16), pltpu.VMEM(wdn.shape[1:], BF16)]
  blocks += sum(_nbytes(w.shape[1:], BF16) + 2 * _nbytes(w.shape[1:], F32) // CAST_STEPS
                for w in chunked)
  if final_g is not None:
    args.append(final_g)
    in_specs.append(_resident(final_g.shape))
  temps = _nbytes((rows, hidden), F32) + _nbytes((rows, hidden), BF16) + 2 * _nbytes((rows, d), F32)
  return pl.pallas_call(
      functools.partial(_ffn_kernel, mla_out=mla_out is not None, final_norm=final_g is not None),
      grid=(CAST_STEPS + t // rows,),
      in_specs=in_specs,
      out_specs=_tile_spec(rows, d),
      out_shape=jax.ShapeDtypeStruct((t, d), F32),
      scratch_shapes=scratch,
      compiler_params=pltpu.CompilerParams(
          dimension_semantics=("arbitrary",),
          vmem_limit_bytes=_vmem_limit(blocks, temps)),
      name="ffn",
  )(*args)


def kernel(x, positions, norm_mix, norm_ffn, final_norm, mla_w_dkv, mla_q_norm, mla_kv_norm, mla_w_uq, mla_w_ukv, mla_w_o, sgu_w_in, sgu_ln_g, sgu_ln_b, sgu_w_spatial, sgu_b_spatial, sgu_w_out, ffn_w_up, ffn_w_down):
  batch, seq, d = x.shape
  depth = norm_mix.shape[0]
  t = batch * seq
  assert seq % MLP_ROWS == 0 and seq % ATTN_Q == 0 and t % ROPE_TABLE_COLS == 0
  xf = x.reshape(t, d)

  inv_freq = ROPE_THETA ** (-jnp.arange(0, QK_ROPE_DIM, 2, dtype=F32) / QK_ROPE_DIM)
  cos_t, sin_t = _rope_tables(positions.astype(F32).reshape(1, t), inv_freq.reshape(ROPE_HALF, 1))

  row = lambda a: a.reshape(1, -1)
  for i in range(depth):
    j = i // 2
    mla_out = None
    if i % 2 == 0:
      wdkv = jnp.pad(mla_w_dkv[j], ((0, 0), (0, V7X_LANES - QK_ROPE_DIM))).astype(BF16)
      wuq = mla_w_uq[j].reshape(Q_LORA_RANK, MLA_HEADS, QK_HEAD_DIM)
      wqnt = wuq[:, :, :QK_NOPE_DIM].reshape(Q_LORA_RANK, -1).T.astype(BF16)
      wqrt = wuq[:, :, QK_NOPE_DIM:].reshape(Q_LORA_RANK, -1).T.astype(BF16)
      wukv = mla_w_ukv[j].reshape(KV_LORA_RANK, MLA_HEADS, QK_NOPE_DIM + V_HEAD_DIM)
      wuk = wukv[:, :, :QK_NOPE_DIM].transpose(1, 0, 2).astype(BF16)
      wuv = wukv[:, :, QK_NOPE_DIM:].transpose(1, 0, 2).astype(BF16)
      q_t, kc, ct = _mla_proj(xf, row(norm_mix[i]), wdkv, row(mla_q_norm[j]),
                              row(mla_kv_norm[j]), wqnt, wqrt, wuk, cos_t, sin_t)
      o_lat = _mla_attn(q_t, kc, ct, batch, seq)
      mla_out = (o_lat.reshape(t, -1), wuv, mla_w_o, j)
    else:
      xf = _sgu(xf, row(norm_mix[i]), sgu_w_in, row(sgu_ln_g[j]), row(sgu_ln_b[j]),
                sgu_w_spatial, sgu_b_spatial[j].T, sgu_w_out, j)
    last = i == depth - 1
    xf = _ffn(xf, row(norm_ffn[i]), ffn_w_up, ffn_w_down, i,
              mla_out=mla_out, final_g=row(final_norm) if last else None)
  return xf.reshape(batch, seq, d)
```

```python
import functools
import math

import jax
import jax.numpy as jnp
from jax import lax
from jax.experimental import pallas as pl
from jax.experimental.pallas import tpu as pltpu

F32 = jnp.float32
BF16 = jnp.bfloat16

MLA_HEADS = 8
QK_NOPE_DIM = 128
QK_ROPE_DIM = 64
QK_HEAD_DIM = QK_NOPE_DIM + QK_ROPE_DIM
V_HEAD_DIM = 128
Q_LORA_RANK = 256
KV_LORA_RANK = 128
ROPE_THETA = 10000.0
SGU_CHUNK = 128
SGU_GROUPS = 8
NORM_EPS = 1e-6
LN_EPS = 1e-5

V7X_LANES = 128
V7X_BF16_SUBLANES = 16
V7X_VMEM_BYTES = 64 * 1024 * 1024
V7X_VMEM_RESERVE_BYTES = 8 * 1024 * 1024

QK_PAD = KV_LORA_RANK + V7X_LANES
ROPE_HALF = QK_ROPE_DIM // 2
V_ROWS = KV_LORA_RANK + V7X_BF16_SUBLANES

ATTN_Q = 512
ATTN_K = 256
MLP_ROWS = 512
SGU_SUB_ROWS = 256
CAST_STEPS = 8


def _vmem_limit(block_bytes, temp_bytes):
  want = block_bytes + temp_bytes + V7X_VMEM_RESERVE_BYTES
  return int(min(want, V7X_VMEM_BYTES - V7X_VMEM_RESERVE_BYTES))


def _nbytes(shape, dtype):
  return math.prod(shape) * jnp.dtype(dtype).itemsize


def _dot(a, b):
  return jnp.dot(a, b, preferred_element_type=F32)


def _rms(x, g):
  return x * lax.rsqrt(jnp.mean(x * x, axis=-1, keepdims=True) + NORM_EPS) * g


def _resident(shape):
  nd = len(shape)
  return pl.BlockSpec(shape, lambda *_: (0,) * nd, pipeline_mode=pl.Buffered(1))


def _layer_resident(stacked, layer):
  nd = stacked.ndim
  return pl.BlockSpec((1,) + stacked.shape[1:], lambda *_: (layer,) + (0,) * (nd - 1),
                      pipeline_mode=pl.Buffered(1))


def _rope_t(x_t, cos_t, sin_t):
  x1, x2 = x_t[:ROPE_HALF], x_t[ROPE_HALF:]
  return jnp.concatenate([x1 * cos_t - x2 * sin_t, x2 * cos_t + x1 * sin_t], axis=0)


def _mla_proj_kernel(x_ref, g_ref, wdkv_ref, qn_ref, kvn_ref, wqnt_ref, wqrt_ref,
                     wuk_ref, pos_ref, freq_ref, qt_ref, kc_ref, ct_ref, wql_ref):
  rows = x_ref.shape[0]
  scale = QK_HEAD_DIM ** -0.5 * math.log2(math.e)

  @pl.when(pl.program_id(0) == 0)
  def _():
    for hd in range(MLA_HEADS):
      rws = slice(hd * QK_NOPE_DIM, (hd + 1) * QK_NOPE_DIM)
      wql_ref[rws, :] = (_dot(wuk_ref[hd], wqnt_ref[rws, :]) * scale).astype(BF16)

  h = _rms(x_ref[...], g_ref[0]).astype(BF16)
  lat = _dot(h, wdkv_ref[...])
  c_q = _rms(lat[:, :Q_LORA_RANK], qn_ref[0])
  c_kv = _rms(lat[:, Q_LORA_RANK:Q_LORA_RANK + KV_LORA_RANK], kvn_ref[0])
  ang = freq_ref[...] * pos_ref[...]
  cos_t = jnp.cos(ang)
  sin_t = jnp.sin(ang)
  pad = jnp.zeros((V7X_LANES - QK_ROPE_DIM, rows), F32)

  kr_t = lat[:, Q_LORA_RANK + KV_LORA_RANK:].T
  kr_roped = jnp.concatenate([_rope_t(kr_t[:QK_ROPE_DIM], cos_t, sin_t), pad], axis=0).T
  kc_ref[...] = jnp.concatenate([c_kv.astype(BF16), kr_roped.astype(BF16)], axis=1)
  ct_ref[0] = jnp.concatenate(
      [c_kv.T, jnp.ones((V_ROWS - KV_LORA_RANK, rows), F32)], axis=0).astype(BF16)

  cq_t = c_q.T.astype(BF16)
  ql_t = _dot(wql_ref[...], cq_t)
  qr_t = _dot(wqrt_ref[...], cq_t)
  for hd in range(MLA_HEADS):
    q_lat = ql_t[hd * KV_LORA_RANK:(hd + 1) * KV_LORA_RANK]
    q_rope = _rope_t(qr_t[hd * QK_ROPE_DIM:(hd + 1) * QK_ROPE_DIM], cos_t, sin_t) * scale
    qt_ref[0, :, hd * rows:(hd + 1) * rows] = jnp.concatenate(
        [q_lat, q_rope, pad], axis=0).astype(BF16)


def _mla_proj(x, g, wdkv, qn, kvn, wqnt, wqrt, wuk, pos_row, freq_col):
  t, d = x.shape
  rows = ATTN_Q
  steps = t // rows
  row_spec = lambda width: pl.BlockSpec((rows, width), lambda i: (i, 0))
  in_specs = [row_spec(d), _layer_resident(*g), _resident(wdkv.shape), _layer_resident(*qn),
              _layer_resident(*kvn), _resident(wqnt.shape), _resident(wqrt.shape),
              _resident(wuk.shape), pl.BlockSpec((1, rows), lambda i: (0, i)),
              _resident(freq_col.shape)]
  out_shape = [
      jax.ShapeDtypeStruct((steps, QK_PAD, MLA_HEADS * rows), BF16),
      jax.ShapeDtypeStruct((t, QK_PAD), BF16),
      jax.ShapeDtypeStruct((steps, V_ROWS, rows), BF16),
  ]
  out_specs = [
      pl.BlockSpec((1, QK_PAD, MLA_HEADS * rows), lambda i: (i, 0, 0)),
      row_spec(QK_PAD),
      pl.BlockSpec((1, V_ROWS, rows), lambda i: (i, 0, 0)),
  ]
  weights = sum(_nbytes(w.shape, w.dtype) for w in (wdkv, wqnt, wqrt, wuk))
  blocks = weights + 2 * (_nbytes((rows, d), F32)
                          + _nbytes((MLA_HEADS + 2, rows, QK_PAD), BF16))
  temps = 6 * _nbytes((rows, MLA_HEADS * QK_HEAD_DIM), F32)
  return pl.pallas_call(
      _mla_proj_kernel,
      grid=(steps,),
      in_specs=in_specs,
      out_specs=out_specs,
      out_shape=out_shape,
      scratch_shapes=[pltpu.VMEM((MLA_HEADS * KV_LORA_RANK, Q_LORA_RANK), BF16)],
      compiler_params=pltpu.CompilerParams(
          dimension_semantics=("arbitrary",),
          vmem_limit_bytes=_vmem_limit(blocks, temps)),
      name="mla_proj",
  )(x, g[0], wdkv, qn[0], kvn[0], wqnt, wqrt, wuk, pos_row, freq_col)


_FULL, _DIAG, _SKIP = "full", "diag", "skip"


def _mla_attn_kernel(qt_ref, kc_ref, ct_ref, qn_ref, kn_ref, o_ref, s0_ref, s1_ref,
                     mx0_ref, mx1_ref, m_ref, acc_ref):
  qi = pl.program_id(1)
  tk = ATTN_K
  tq = qt_ref.shape[3] // MLA_HEADS
  units = qt_ref.shape[3] // tk
  bufs = ((s0_ref, mx0_ref), (s1_ref, mx1_ref))

  m_ref[...] = jnp.full(m_ref.shape, -jnp.inf, F32)
  acc_ref[...] = jnp.zeros(acc_ref.shape, F32)

  def keys(kt):
    return kc_ref[0, pl.ds(pl.multiple_of(kt * tk, tk), tk), :]

  def score_unit(k, q_ref, buf, u):
    s_ref, mx_ref = bufs[buf]
    cols = slice(u * tk, (u + 1) * tk)
    s = _dot(k, q_ref[0, 0, :, cols])
    s_ref[:, cols] = s
    mx_ref[:, cols] = jnp.max(s, axis=0, keepdims=True)

  def softmax_unit(v_t, buf, u, mode):
    s_ref, mx_ref = bufs[buf]
    cols = slice(u * tk, (u + 1) * tk)
    s = s_ref[:, cols]
    if mode == _DIAG:
      key = lax.broadcasted_iota(jnp.int32, s.shape, 0)
      qry = lax.broadcasted_iota(jnp.int32, s.shape, 1)
      s = jnp.where(key <= qry, s, -jnp.inf)
      mx = jnp.max(s, axis=0, keepdims=True)
    else:
      mx = mx_ref[:, cols]
    m_prev = m_ref[:, cols]
    m_new = jnp.maximum(m_prev, mx)
    alpha = jnp.exp2(m_prev - m_new)
    p = jnp.exp2(s - m_new).astype(BF16)
    acc_ref[:, cols] = alpha * acc_ref[:, cols] + _dot(v_t, p)
    m_ref[:, cols] = m_new

  def step(nxt, next_units, sb, half, buf, mode_of_unit):
    v_t = None if sb is None else ct_ref[0, sb, :, half * tk:(half + 1) * tk]
    for u in range(units):
      if u in next_units:
        score_unit(nxt[0], nxt[1], 1 - buf, u)
      if sb is not None and mode_of_unit(u) != _SKIP:
        softmax_unit(v_t, buf, u, mode_of_unit(u))

  all_units = range(units)

  @pl.when(jnp.logical_and(pl.program_id(0) == 0, qi == 0))
  def _():
    step((keys(0), qt_ref), all_units, None, None, 1, None)

  def body(j, carry):
    step((keys(2 * j + 1), qt_ref), all_units, j, 0, 0, lambda u: _FULL)
    step((keys(2 * j + 2), qt_ref), all_units, j, 1, 1, lambda u: _FULL)
    return carry

  lax.fori_loop(0, qi, body, 0)
  step((keys(2 * qi + 1), qt_ref), range(1, units, 2), qi, 0, 0,
       lambda u: _DIAG if u % 2 == 0 else _FULL)
  step((kn_ref[0], qn_ref), all_units, qi, 1, 1, lambda u: _SKIP if u % 2 == 0 else _DIAG)

  o_t = acc_ref[:KV_LORA_RANK, :] / acc_ref[KV_LORA_RANK:KV_LORA_RANK + 1, :]
  for hd in range(MLA_HEADS):
    o_ref[0, :, hd * KV_LORA_RANK:(hd + 1) * KV_LORA_RANK] = (
        o_t[:, hd * tq:(hd + 1) * tq].T.astype(o_ref.dtype))


def _mla_attn(q_t, kc, ct, batch, seq):
  tq, tk = ATTN_Q, ATTN_K
  assert tq == 2 * tk
  nq = MLA_HEADS * tq
  q_tiles = seq // tq
  qt4 = q_t.reshape(batch, q_tiles, QK_PAD, nq)
  kc3 = kc.reshape(batch, seq, QK_PAD)
  ct4 = ct.reshape(batch, q_tiles, V_ROWS, tq)
  out_w = MLA_HEADS * KV_LORA_RANK
  blocks = 2 * (2 * _nbytes((QK_PAD, nq), BF16) + _nbytes((seq + tk, QK_PAD), BF16)
                + _nbytes((V_ROWS, seq), BF16) + _nbytes((tq, out_w), BF16))

  def next_tile(b, i):
    flat = jnp.minimum(b * q_tiles + i + 1, batch * q_tiles - 1)
    return flat // q_tiles, flat % q_tiles

  scratch = _nbytes((2 * tk + V_ROWS + 24, nq), F32)
  temps = 2 * _nbytes((tk, nq), F32)
  return pl.pallas_call(
      _mla_attn_kernel,
      grid=(batch, q_tiles),
      in_specs=[
          pl.BlockSpec((1, 1, QK_PAD, nq), lambda b, i: (b, i, 0, 0)),
          pl.BlockSpec((1, seq, QK_PAD), lambda b, i: (b, 0, 0)),
          pl.BlockSpec((1, q_tiles, V_ROWS, tq), lambda b, i: (b, 0, 0, 0)),
          pl.BlockSpec((1, 1, QK_PAD, nq), lambda b, i: (*next_tile(b, i), 0, 0)),
          pl.BlockSpec((1, tk, QK_PAD), lambda b, i: (next_tile(b, i)[0], 0, 0)),
      ],
      out_specs=pl.BlockSpec((1, tq, out_w), lambda b, i: (b, i, 0)),
      out_shape=jax.ShapeDtypeStruct((batch, seq, out_w), BF16),
      scratch_shapes=[pltpu.VMEM((tk, nq), F32), pltpu.VMEM((tk, nq), F32),
                      pltpu.VMEM((1, nq), F32), pltpu.VMEM((1, nq), F32),
                      pltpu.VMEM((1, nq), F32), pltpu.VMEM((V_ROWS, nq), F32)],
      compiler_params=pltpu.CompilerParams(
          dimension_semantics=("arbitrary", "arbitrary"),
          vmem_limit_bytes=_vmem_limit(blocks + scratch, temps)),
      name="mla_attn",
  )(qt4, kc3, ct4, qt4, kc3)


def _cast_chunk_spec(stacked, layer):
  _, r, c = stacked.shape
  return pl.BlockSpec((1, r // CAST_STEPS, c),
                      lambda s: (layer, jnp.minimum(s, CAST_STEPS - 1), 0))


def _cast_chunk(step, src_ref, dst_ref):
  chunk = src_ref.shape[1]
  dst_ref[pl.ds(pl.multiple_of(step * chunk, chunk), chunk), :] = src_ref[0].astype(BF16)


def _tile_spec(rows, width):
  return pl.BlockSpec((rows, width), lambda s: (jnp.maximum(s - CAST_STEPS, 0), 0))


def _sgu_kernel(x_ref, g_ref, win_ref, lng_ref, lnb_ref, wsp_ref, bsp_ref, wout_ref,
                o_ref, win_bf, wout_bf, gated_ref):
  step = pl.program_id(0)

  @pl.when(step < CAST_STEPS)
  def _():
    _cast_chunk(step, win_ref, win_bf)
    _cast_chunk(step, wout_ref, wout_bf)

  @pl.when(step >= CAST_STEPS)
  def _():
    rows = x_ref.shape[0]
    width = lng_ref.shape[2]
    group_dim = width // SGU_GROUPS
    t_idx = lax.broadcasted_iota(jnp.int32, (SGU_CHUNK, SGU_CHUNK), 0)
    s_idx = lax.broadcasted_iota(jnp.int32, (SGU_CHUNK, SGU_CHUNK), 1)
    causal = s_idx <= t_idx
    w_sp = [jnp.where(causal, wsp_ref[0, gi], 0.0).astype(BF16)
            for gi in range(SGU_GROUPS)]

    subs = [slice(r0, r0 + SGU_SUB_ROWS) for r0 in range(0, rows, SGU_SUB_ROWS)]
    zs = [_dot(_rms(x_ref[sub, :], g_ref[0]).astype(BF16), win_bf[...]) for sub in subs]
    for sub, z in zip(subs, zs):
      r0 = sub.start
      z = 0.5 * z * (1.0 + lax.erf(z * (0.5 ** 0.5)))
      u = z[:, :width]
      v = z[:, width:]
      mu = jnp.mean(v, axis=-1, keepdims=True)
      var = jnp.mean(jnp.square(v - mu), axis=-1, keepdims=True)
      v = ((v - mu) * lax.rsqrt(var + LN_EPS) * lng_ref[0] + lnb_ref[0]).astype(BF16)
      for gi in range(SGU_GROUPS):
        bias = jnp.broadcast_to(bsp_ref[0, :, gi:gi + 1], (SGU_CHUNK, group_dim))
        cols = slice(gi * group_dim, (gi + 1) * group_dim)
        for c0 in range(0, SGU_SUB_ROWS, SGU_CHUNK):
          rws = slice(c0, c0 + SGU_CHUNK)
          mixed = _dot(w_sp[gi], v[rws, cols]) + bias
          gated_ref[r0 + c0:r0 + c0 + SGU_CHUNK, cols] = (u[rws, cols] * mixed).astype(BF16)
      o_ref[sub, :] = x_ref[sub, :] + _dot(gated_ref[sub, :], wout_bf[...])


def _sgu(x, g, win, lng, lnb, wsp, bsp_t, wout, layer):
  t, d = x.shape
  rows = MLP_ROWS
  width = lng.shape[2]
  scratch = [pltpu.VMEM(win.shape[1:], BF16), pltpu.VMEM(wout.shape[1:], BF16),
             pltpu.VMEM((rows, width), BF16)]
  blocks = (2 * (_nbytes(win.shape[1:], F32) + _nbytes(wout.shape[1:], F32)) // CAST_STEPS
            + _nbytes(win.shape[1:], BF16) + _nbytes(wout.shape[1:], BF16)
            + _nbytes(wsp.shape[1:], F32)
            + 4 * _nbytes((rows, d), F32) + _nbytes((rows, width), BF16))
  temps = 3 * _nbytes((rows, 2 * width), F32)
  return pl.pallas_call(
      _sgu_kernel,
      grid=(CAST_STEPS + t // rows,),
      in_specs=[_tile_spec(rows, d), _layer_resident(*g),
                _cast_chunk_spec(win, layer), _layer_resident(lng, layer),
                _layer_resident(lnb, layer), _layer_resident(wsp, layer),
                _layer_resident(bsp_t, layer), _cast_chunk_spec(wout, layer)],
      out_specs=_tile_spec(rows, d),
      out_shape=jax.ShapeDtypeStruct((t, d), F32),
      scratch_shapes=scratch,
      compiler_params=pltpu.CompilerParams(
          dimension_semantics=("arbitrary",),
          vmem_limit_bytes=_vmem_limit(blocks, temps)),
      name="sgu_mixer",
  )(x, g[0], win, lng, lnb, wsp, bsp_t, wout)


def _ffn_kernel(*refs, mla_out, final_norm):
  refs = list(refs)
  x_ref = refs.pop(0)
  if mla_out:
    ol_ref, wuv_ref, wo_ref = refs[:3]
    del refs[:3]
  g_ref, wup_ref, wdn_ref = refs[:3]
  del refs[:3]
  gf_ref = refs.pop(0) if final_norm else None
  o_ref = refs.pop(0)
  wo_bf = refs.pop(0) if mla_out else None
  wup_bf, wdn_bf = refs
  step = pl.program_id(0)

  @pl.when(step < CAST_STEPS)
  def _():
    _cast_chunk(step, wup_ref, wup_bf)
    _cast_chunk(step, wdn_ref, wdn_bf)
    if mla_out:
      rws = pl.ds(pl.multiple_of(step * V_HEAD_DIM, V_HEAD_DIM), V_HEAD_DIM)
      wo_bf[rws, :] = _dot(wuv_ref[step], wo_ref[0].astype(BF16)).astype(BF16)

  @pl.when(step >= CAST_STEPS)
  def _():
    x = x_ref[...]
    if mla_out:
      x = x + _dot(ol_ref[...], wo_bf[...])
    h = _rms(x, g_ref[0]).astype(BF16)
    a = jnp.square(jnp.maximum(_dot(h, wup_bf[...]), 0.0)).astype(BF16)
    y = x + _dot(a, wdn_bf[...])
    if final_norm:
      y = _rms(y, gf_ref[0])
    o_ref[...] = y


def _ffn(x, g, wup, wdn, layer, mla_out=None, final_g=None):
  t, d = x.shape
  rows = MLP_ROWS
  hidden = wup.shape[2]
  args = [x]
  in_specs = [_tile_spec(rows, d)]
  scratch = []
  blocks = 4 * _nbytes((rows, d), F32)
  chunked = [wup, wdn]
  if mla_out is not None:
    o_lat, wuv, wo, wo_layer = mla_out
    assert CAST_STEPS == MLA_HEADS and wo.shape[1] == MLA_HEADS * V_HEAD_DIM
    args += [o_lat, wuv, wo]
    in_specs += [_tile_spec(rows, o_lat.shape[1]), _resident(wuv.shape),
                 _cast_chunk_spec(wo, wo_layer)]
    scratch.append(pltpu.VMEM(wo.shape[1:], BF16))
    blocks += 2 * _nbytes((rows, o_lat.shape[1]), BF16) + _nbytes(wuv.shape, BF16)
    chunked.append(wo)
  args += [g, wup, wdn]
  in_specs += [_layer_resident(g, layer), _cast_chunk_spec(wup, layer), _cast_chunk_spec(wdn, layer)]
  scratch += [pltpu.VMEM(wup.shape[1:], BF16), pltpu.VMEM(wdn.shape[1:], BF16)]
  blocks += sum(_nbytes(w.shape[1:], BF16) + 2 * _nbytes(w.shape[1:], F32) // CAST_STEPS
                for w in chunked)
  if final_g is not None:
    args.append(final_g)
    in_specs.append(_layer_resident(final_g, 0))
  temps = _nbytes((rows, hidden), F32) + _nbytes((rows, hidden), BF16) + 2 * _nbytes((rows, d), F32)
  return pl.pallas_call(
      functools.partial(_ffn_kernel, mla_out=mla_out is not None, final_norm=final_g is not None),
      grid=(CAST_STEPS + t // rows,),
      in_specs=in_specs,
      out_specs=_tile_spec(rows, d),
      out_shape=jax.ShapeDtypeStruct((t, d), F32),
      scratch_shapes=scratch,
      compiler_params=pltpu.CompilerParams(
          dimension_semantics=("arbitrary",),
          vmem_limit_bytes=_vmem_limit(blocks, temps)),
      name="ffn",
  )(*args)


def kernel(x, positions, norm_mix, norm_ffn, final_norm, mla_w_dkv, mla_q_norm, mla_kv_norm, mla_w_uq, mla_w_ukv, mla_w_o, sgu_w_in, sgu_ln_g, sgu_ln_b, sgu_w_spatial, sgu_b_spatial, sgu_w_out, ffn_w_up, ffn_w_down):
  batch, seq, d = x.shape
  depth = norm_mix.shape[0]
  t = batch * seq
  assert seq % MLP_ROWS == 0 and seq % ATTN_Q == 0
  xf = x.reshape(t, d)

  inv_freq = ROPE_THETA ** (-jnp.arange(0, QK_ROPE_DIM, 2, dtype=F32) / QK_ROPE_DIM)
  freq_col = inv_freq.reshape(ROPE_HALF, 1)
  pos_row = positions.astype(F32).reshape(1, t)

  vec = lambda a: a.reshape(a.shape[0], 1, a.shape[1])
  g_mix, g_ffn, g_final = vec(norm_mix), vec(norm_ffn), final_norm.reshape(1, 1, d)
  q_norm, kv_norm = vec(mla_q_norm), vec(mla_kv_norm)
  ln_g, ln_b = vec(sgu_ln_g), vec(sgu_ln_b)
  b_spatial_t = sgu_b_spatial.transpose(0, 2, 1)
  for i in range(depth):
    j = i // 2
    mla_out = None
    if i % 2 == 0:
      wdkv = jnp.pad(mla_w_dkv[j], ((0, 0), (0, V7X_LANES - QK_ROPE_DIM))).astype(BF16)
      wuq = mla_w_uq[j].reshape(Q_LORA_RANK, MLA_HEADS, QK_HEAD_DIM)
      wqnt = wuq[:, :, :QK_NOPE_DIM].reshape(Q_LORA_RANK, -1).T.astype(BF16)
      wqrt = wuq[:, :, QK_NOPE_DIM:].reshape(Q_LORA_RANK, -1).T.astype(BF16)
      wukv = mla_w_ukv[j].reshape(KV_LORA_RANK, MLA_HEADS, QK_NOPE_DIM + V_HEAD_DIM)
      wuk = wukv[:, :, :QK_NOPE_DIM].transpose(1, 0, 2).astype(BF16)
      wuv = wukv[:, :, QK_NOPE_DIM:].transpose(1, 0, 2).astype(BF16)
      q_t, kc, ct = _mla_proj(xf, (g_mix, i), wdkv, (q_norm, j), (kv_norm, j),
                              wqnt, wqrt, wuk, pos_row, freq_col)
      o_lat = _mla_attn(q_t, kc, ct, batch, seq)
      mla_out = (o_lat.reshape(t, -1), wuv, mla_w_o, j)
    else:
      xf = _sgu(xf, (g_mix, i), sgu_w_in, ln_g, ln_b, sgu_w_spatial, b_spatial_t,
                sgu_w_out, j)
    last = i == depth - 1
    xf = _ffn(xf, g_ffn, ffn_w_up, ffn_w_down, i,
              mla_out=mla_out, final_g=g_final if last else None)
  return xf.reshape(batch, seq, d)
```

```python
import functools
import math

import jax
import jax.numpy as jnp
from jax import lax
from jax.experimental import pallas as pl
from jax.experimental.pallas import tpu as pltpu

F32 = jnp.float32
BF16 = jnp.bfloat16

MLA_HEADS = 8
QK_NOPE_DIM = 128
QK_ROPE_DIM = 64
QK_HEAD_DIM = QK_NOPE_DIM + QK_ROPE_DIM
V_HEAD_DIM = 128
Q_LORA_RANK = 256
KV_LORA_RANK = 128
ROPE_THETA = 10000.0
SGU_CHUNK = 128
SGU_GROUPS = 8
NORM_EPS = 1e-6
LN_EPS = 1e-5

V7X_LANES = 128
V7X_BF16_SUBLANES = 16
V7X_VMEM_BYTES = 64 * 1024 * 1024
V7X_VMEM_RESERVE_BYTES = 8 * 1024 * 1024

QK_PAD = KV_LORA_RANK + V7X_LANES
ROPE_HALF = QK_ROPE_DIM // 2
V_ROWS = KV_LORA_RANK + V7X_BF16_SUBLANES

ATTN_Q = 512
ATTN_K = 256
PROJ_SUB_ROWS = 256
MLP_ROWS = 512
SGU_IN_COLS = 512
CAST_STEPS = 8


def _vmem_limit(block_bytes, temp_bytes):
  want = block_bytes + temp_bytes + V7X_VMEM_RESERVE_BYTES
  return int(min(want, V7X_VMEM_BYTES - V7X_VMEM_RESERVE_BYTES))


def _nbytes(shape, dtype):
  return math.prod(shape) * jnp.dtype(dtype).itemsize


def _dot(a, b):
  return jnp.dot(a, b, preferred_element_type=F32)


def _rms(x, g):
  return x * lax.rsqrt(jnp.mean(x * x, axis=-1, keepdims=True) + NORM_EPS) * g


def _resident(shape):
  nd = len(shape)
  return pl.BlockSpec(shape, lambda *_: (0,) * nd, pipeline_mode=pl.Buffered(1))


def _layer_resident(stacked, layer):
  nd = stacked.ndim
  return pl.BlockSpec((1,) + stacked.shape[1:], lambda *_: (layer,) + (0,) * (nd - 1),
                      pipeline_mode=pl.Buffered(1))


def _rope_t(x_t, cos_t, sin_t):
  x1, x2 = x_t[:ROPE_HALF], x_t[ROPE_HALF:]
  return jnp.concatenate([x1 * cos_t - x2 * sin_t, x2 * cos_t + x1 * sin_t], axis=0)


def _mla_proj_kernel(x_ref, g_ref, wdkv_ref, qn_ref, kvn_ref, wqnt_ref, wqrt_ref,
                     wuk_ref, pos_ref, freq_ref, qt_ref, kc_ref, ct_ref, wql_ref):
  rows = x_ref.shape[0]
  scale = QK_HEAD_DIM ** -0.5 * math.log2(math.e)

  @pl.when(pl.program_id(0) == 0)
  def _():
    for hd in range(MLA_HEADS):
      rws = slice(hd * QK_NOPE_DIM, (hd + 1) * QK_NOPE_DIM)
      wql_ref[rws, :] = (_dot(wuk_ref[hd], wqnt_ref[rws, :]) * scale).astype(BF16)

  subs = [slice(r0, r0 + PROJ_SUB_ROWS) for r0 in range(0, rows, PROJ_SUB_ROWS)]
  pad = jnp.zeros((V7X_LANES - QK_ROPE_DIM, PROJ_SUB_ROWS), F32)
  lats = [_dot(_rms(x_ref[sub, :], g_ref[0]).astype(BF16), wdkv_ref[...])
          for sub in subs]
  tables = []
  for sub in subs:
    ang = freq_ref[...] * pos_ref[:, sub]
    tables.append((jnp.cos(ang), jnp.sin(ang)))

  cq_ts = []
  for sub, lat, (cos_t, sin_t) in zip(subs, lats, tables):
    c_q = _rms(lat[:, :Q_LORA_RANK], qn_ref[0])
    c_kv = _rms(lat[:, Q_LORA_RANK:Q_LORA_RANK + KV_LORA_RANK], kvn_ref[0])
    kr_t = lat[:, Q_LORA_RANK + KV_LORA_RANK:].T
    kr_roped = jnp.concatenate([_rope_t(kr_t[:QK_ROPE_DIM], cos_t, sin_t), pad], axis=0).T
    kc_ref[sub, :] = jnp.concatenate([c_kv.astype(BF16), kr_roped.astype(BF16)], axis=1)
    ct_ref[0, :, sub] = jnp.concatenate(
        [c_kv.T, jnp.ones((V_ROWS - KV_LORA_RANK, PROJ_SUB_ROWS), F32)], axis=0).astype(BF16)
    cq_ts.append(c_q.T.astype(BF16))

  qs = [(_dot(wql_ref[...], cq_t), _dot(wqrt_ref[...], cq_t)) for cq_t in cq_ts]
  for sub, (ql_t, qr_t), (cos_t, sin_t) in zip(subs, qs, tables):
    for hd in range(MLA_HEADS):
      q_lat = ql_t[hd * KV_LORA_RANK:(hd + 1) * KV_LORA_RANK]
      q_rope = _rope_t(qr_t[hd * QK_ROPE_DIM:(hd + 1) * QK_ROPE_DIM], cos_t, sin_t) * scale
      qt_ref[0, :, hd * rows + sub.start:hd * rows + sub.stop] = jnp.concatenate(
          [q_lat, q_rope, pad], axis=0).astype(BF16)


def _mla_proj(x, g, wdkv, qn, kvn, wqnt, wqrt, wuk, pos_row, freq_col):
  t, d = x.shape
  rows = ATTN_Q
  steps = t // rows
  row_spec = lambda width: pl.BlockSpec((rows, width), lambda i: (i, 0))
  in_specs = [row_spec(d), _layer_resident(*g), _resident(wdkv.shape), _layer_resident(*qn),
              _layer_resident(*kvn), _resident(wqnt.shape), _resident(wqrt.shape),
              _resident(wuk.shape), pl.BlockSpec((1, rows), lambda i: (0, i)),
              _resident(freq_col.shape)]
  out_shape = [
      jax.ShapeDtypeStruct((steps, QK_PAD, MLA_HEADS * rows), BF16),
      jax.ShapeDtypeStruct((t, QK_PAD), BF16),
      jax.ShapeDtypeStruct((steps, V_ROWS, rows), BF16),
  ]
  out_specs = [
      pl.BlockSpec((1, QK_PAD, MLA_HEADS * rows), lambda i: (i, 0, 0)),
      row_spec(QK_PAD),
      pl.BlockSpec((1, V_ROWS, rows), lambda i: (i, 0, 0)),
  ]
  weights = sum(_nbytes(w.shape, w.dtype) for w in (wdkv, wqnt, wqrt, wuk))
  blocks = weights + 2 * (_nbytes((rows, d), F32)
                          + _nbytes((MLA_HEADS + 2, rows, QK_PAD), BF16))
  temps = 6 * _nbytes((rows, MLA_HEADS * QK_HEAD_DIM), F32)
  return pl.pallas_call(
      _mla_proj_kernel,
      grid=(steps,),
      in_specs=in_specs,
      out_specs=out_specs,
      out_shape=out_shape,
      scratch_shapes=[pltpu.VMEM((MLA_HEADS * KV_LORA_RANK, Q_LORA_RANK), BF16)],
      compiler_params=pltpu.CompilerParams(
          dimension_semantics=("arbitrary",),
          vmem_limit_bytes=_vmem_limit(blocks, temps)),
      name="mla_proj",
  )(x, g[0], wdkv, qn[0], kvn[0], wqnt, wqrt, wuk, pos_row, freq_col)


_FULL, _DIAG, _SKIP = "full", "diag", "skip"


def _mla_attn_kernel(qt_ref, kc_ref, ct_ref, qn_ref, kn_ref, o_ref, s0_ref, s1_ref,
                     mx0_ref, mx1_ref, m_ref, acc_ref):
  qi = pl.program_id(1)
  tk = ATTN_K
  tq = qt_ref.shape[3] // MLA_HEADS
  units = qt_ref.shape[3] // tk
  bufs = ((s0_ref, mx0_ref), (s1_ref, mx1_ref))

  m_ref[...] = jnp.full(m_ref.shape, -jnp.inf, F32)
  acc_ref[...] = jnp.zeros(acc_ref.shape, F32)

  def keys(kt):
    return kc_ref[0, pl.ds(pl.multiple_of(kt * tk, tk), tk), :]

  def score_unit(k, q_ref, buf, u):
    s_ref, mx_ref = bufs[buf]
    cols = slice(u * tk, (u + 1) * tk)
    s = _dot(k, q_ref[0, 0, :, cols])
    s_ref[:, cols] = s
    mx_ref[:, cols] = jnp.max(s, axis=0, keepdims=True)

  def softmax_unit(v_t, buf, u, mode):
    s_ref, mx_ref = bufs[buf]
    cols = slice(u * tk, (u + 1) * tk)
    s = s_ref[:, cols]
    if mode == _DIAG:
      key = lax.broadcasted_iota(jnp.int32, s.shape, 0)
      qry = lax.broadcasted_iota(jnp.int32, s.shape, 1)
      s = jnp.where(key <= qry, s, -jnp.inf)
      mx = jnp.max(s, axis=0, keepdims=True)
    else:
      mx = mx_ref[:, cols]
    m_prev = m_ref[:, cols]
    m_new = jnp.maximum(m_prev, mx)
    alpha = jnp.exp2(m_prev - m_new)
    p = jnp.exp2(s - m_new).astype(BF16)
    acc_ref[:, cols] = alpha * acc_ref[:, cols] + _dot(v_t, p)
    m_ref[:, cols] = m_new

  def step(nxt, next_units, sb, half, buf, mode_of_unit):
    v_t = None if sb is None else ct_ref[0, sb, :, half * tk:(half + 1) * tk]
    for u in range(units):
      if u in next_units:
        score_unit(nxt[0], nxt[1], 1 - buf, u)
      if sb is not None and mode_of_unit(u) != _SKIP:
        softmax_unit(v_t, buf, u, mode_of_unit(u))

  all_units = range(units)

  @pl.when(jnp.logical_and(pl.program_id(0) == 0, qi == 0))
  def _():
    step((keys(0), qt_ref), all_units, None, None, 1, None)

  def body(j, carry):
    step((keys(2 * j + 1), qt_ref), all_units, j, 0, 0, lambda u: _FULL)
    step((keys(2 * j + 2), qt_ref), all_units, j, 1, 1, lambda u: _FULL)
    return carry

  lax.fori_loop(0, qi, body, 0)
  step((keys(2 * qi + 1), qt_ref), range(1, units, 2), qi, 0, 0,
       lambda u: _DIAG if u % 2 == 0 else _FULL)
  step((kn_ref[0], qn_ref), all_units, qi, 1, 1, lambda u: _SKIP if u % 2 == 0 else _DIAG)

  o_t = acc_ref[:KV_LORA_RANK, :] / acc_ref[KV_LORA_RANK:KV_LORA_RANK + 1, :]
  for hd in range(MLA_HEADS):
    o_ref[0, :, hd * KV_LORA_RANK:(hd + 1) * KV_LORA_RANK] = (
        o_t[:, hd * tq:(hd + 1) * tq].T.astype(o_ref.dtype))


def _mla_attn(q_t, kc, ct, batch, seq):
  tq, tk = ATTN_Q, ATTN_K
  assert tq == 2 * tk
  nq = MLA_HEADS * tq
  q_tiles = seq // tq
  qt4 = q_t.reshape(batch, q_tiles, QK_PAD, nq)
  kc3 = kc.reshape(batch, seq, QK_PAD)
  ct4 = ct.reshape(batch, q_tiles, V_ROWS, tq)
  out_w = MLA_HEADS * KV_LORA_RANK
  blocks = 2 * (2 * _nbytes((QK_PAD, nq), BF16) + _nbytes((seq + tk, QK_PAD), BF16)
                + _nbytes((V_ROWS, seq), BF16) + _nbytes((tq, out_w), BF16))

  def next_tile(b, i):
    flat = jnp.minimum(b * q_tiles + i + 1, batch * q_tiles - 1)
    return flat // q_tiles, flat % q_tiles

  scratch = _nbytes((2 * tk + V_ROWS + 24, nq), F32)
  temps = 2 * _nbytes((tk, nq), F32)
  return pl.pallas_call(
      _mla_attn_kernel,
      grid=(batch, q_tiles),
      in_specs=[
          pl.BlockSpec((1, 1, QK_PAD, nq), lambda b, i: (b, i, 0, 0)),
          pl.BlockSpec((1, seq, QK_PAD), lambda b, i: (b, 0, 0)),
          pl.BlockSpec((1, q_tiles, V_ROWS, tq), lambda b, i: (b, 0, 0, 0)),
          pl.BlockSpec((1, 1, QK_PAD, nq), lambda b, i: (*next_tile(b, i), 0, 0)),
          pl.BlockSpec((1, tk, QK_PAD), lambda b, i: (next_tile(b, i)[0], 0, 0)),
      ],
      out_specs=pl.BlockSpec((1, tq, out_w), lambda b, i: (b, i, 0)),
      out_shape=jax.ShapeDtypeStruct((batch, seq, out_w), BF16),
      scratch_shapes=[pltpu.VMEM((tk, nq), F32), pltpu.VMEM((tk, nq), F32),
                      pltpu.VMEM((1, nq), F32), pltpu.VMEM((1, nq), F32),
                      pltpu.VMEM((1, nq), F32), pltpu.VMEM((V_ROWS, nq), F32)],
      compiler_params=pltpu.CompilerParams(
          dimension_semantics=("arbitrary", "arbitrary"),
          vmem_limit_bytes=_vmem_limit(blocks + scratch, temps)),
      name="mla_attn",
  )(qt4, kc3, ct4, qt4, kc3)


def _cast_chunk_spec(stacked, layer):
  _, r, c = stacked.shape
  return pl.BlockSpec((1, r // CAST_STEPS, c),
                      lambda s: (layer, jnp.minimum(s, CAST_STEPS - 1), 0))


def _cast_chunk(step, src_ref, dst_ref):
  chunk = src_ref.shape[1]
  dst_ref[pl.ds(pl.multiple_of(step * chunk, chunk), chunk), :] = src_ref[0].astype(BF16)


def _tile_spec(rows, width):
  return pl.BlockSpec((rows, width), lambda s: (jnp.maximum(s - CAST_STEPS, 0), 0))


def _sgu_kernel(x_ref, g_ref, win_ref, lng_ref, lnb_ref, wsp_ref, bsp_ref, wout_ref,
                o_ref, win_bf, wout_bf, gated_ref, act_ref):
  step = pl.program_id(0)

  @pl.when(step < CAST_STEPS)
  def _():
    _cast_chunk(step, win_ref, win_bf)
    _cast_chunk(step, wout_ref, wout_bf)

  @pl.when(step >= CAST_STEPS)
  def _():
    rows = x_ref.shape[0]
    width = lng_ref.shape[2]
    group_dim = width // SGU_GROUPS
    x = x_ref[...]
    h = _rms(x, g_ref[0]).astype(BF16)

    def gelu_chunk(c0):
      cols = slice(c0, c0 + SGU_IN_COLS)
      z = _dot(h, win_bf[:, cols])
      act_ref[:, cols] = 0.5 * z * (1.0 + lax.erf(z * (0.5 ** 0.5)))

    for c0 in range(width, 2 * width, SGU_IN_COLS):
      gelu_chunk(c0)
    v = act_ref[:, width:]
    mu = jnp.mean(v, axis=-1, keepdims=True)
    var = jnp.mean(jnp.square(v - mu), axis=-1, keepdims=True)
    v = ((v - mu) * lax.rsqrt(var + LN_EPS) * lng_ref[0] + lnb_ref[0]).astype(BF16)
    for c0 in range(0, width, SGU_IN_COLS):
      gelu_chunk(c0)

    t_idx = lax.broadcasted_iota(jnp.int32, (SGU_CHUNK, SGU_CHUNK), 0)
    s_idx = lax.broadcasted_iota(jnp.int32, (SGU_CHUNK, SGU_CHUNK), 1)
    causal = s_idx <= t_idx
    for gi in range(SGU_GROUPS):
      w_g = jnp.where(causal, wsp_ref[0, gi], 0.0).astype(BF16)
      bias = jnp.broadcast_to(bsp_ref[0, :, gi:gi + 1], (SGU_CHUNK, group_dim))
      cols = slice(gi * group_dim, (gi + 1) * group_dim)
      for r0 in range(0, rows, SGU_CHUNK):
        rws = slice(r0, r0 + SGU_CHUNK)
        mixed = _dot(w_g, v[rws, cols]) + bias
        gated_ref[rws, cols] = (act_ref[rws, cols] * mixed).astype(BF16)
    o_ref[...] = x + _dot(gated_ref[...], wout_bf[...])


def _sgu(x, g, win, lng, lnb, wsp, bsp_t, wout, layer):
  t, d = x.shape
  rows = MLP_ROWS
  width = lng.shape[2]
  scratch = [pltpu.VMEM(win.shape[1:], BF16), pltpu.VMEM(wout.shape[1:], BF16),
             pltpu.VMEM((rows, width), BF16), pltpu.VMEM((rows, 2 * width), F32)]
  blocks = (2 * (_nbytes(win.shape[1:], F32) + _nbytes(wout.shape[1:], F32)) // CAST_STEPS
            + _nbytes(win.shape[1:], BF16) + _nbytes(wout.shape[1:], BF16)
            + _nbytes(wsp.shape[1:], F32)
            + 4 * _nbytes((rows, d), F32) + _nbytes((rows, width), BF16))
  temps = 3 * _nbytes((rows, 2 * width), F32)
  return pl.pallas_call(
      _sgu_kernel,
      grid=(CAST_STEPS + t // rows,),
      in_specs=[_tile_spec(rows, d), _layer_resident(*g),
                _cast_chunk_spec(win, layer), _layer_resident(lng, layer),
                _layer_resident(lnb, layer), _layer_resident(wsp, layer),
                _layer_resident(bsp_t, layer), _cast_chunk_spec(wout, layer)],
      out_specs=_tile_spec(rows, d),
      out_shape=jax.ShapeDtypeStruct((t, d), F32),
      scratch_shapes=scratch,
      compiler_params=pltpu.CompilerParams(
          dimension_semantics=("arbitrary",),
          vmem_limit_bytes=_vmem_limit(blocks, temps)),
      name="sgu_mixer",
  )(x, g[0], win, lng, lnb, wsp, bsp_t, wout)


def _ffn_kernel(*refs, mla_out, final_norm):
  refs = list(refs)
  x_ref = refs.pop(0)
  if mla_out:
    ol_ref, wuv_ref, wo_ref = refs[:3]
    del refs[:3]
  g_ref, wup_ref, wdn_ref = refs[:3]
  del refs[:3]
  gf_ref = refs.pop(0) if final_norm else None
  o_ref = refs.pop(0)
  wo_bf = refs.pop(0) if mla_out else None
  wup_bf, wdn_bf = refs
  step = pl.program_id(0)

  @pl.when(step < CAST_STEPS)
  def _():
    _cast_chunk(step, wup_ref, wup_bf)
    _cast_chunk(step, wdn_ref, wdn_bf)
    if mla_out:
      rws = pl.ds(pl.multiple_of(step * V_HEAD_DIM, V_HEAD_DIM), V_HEAD_DIM)
      wo_bf[rws, :] = _dot(wuv_ref[step], wo_ref[0].astype(BF16)).astype(BF16)

  @pl.when(step >= CAST_STEPS)
  def _():
    x = x_ref[...]
    if mla_out:
      x = x + _dot(ol_ref[...], wo_bf[...])
    h = _rms(x, g_ref[0]).astype(BF16)
    a = jnp.square(jnp.maximum(_dot(h, wup_bf[...]), 0.0)).astype(BF16)
    y = x + _dot(a, wdn_bf[...])
    if final_norm:
      y = _rms(y, gf_ref[0])
    o_ref[...] = y


def _ffn(x, g, wup, wdn, layer, mla_out=None, final_g=None):
  t, d = x.shape
  rows = MLP_ROWS
  hidden = wup.shape[2]
  args = [x]
  in_specs = [_tile_spec(rows, d)]
  scratch = []
  blocks = 4 * _nbytes((rows, d), F32)
  chunked = [wup, wdn]
  if mla_out is not None:
    o_lat, wuv, wo, wo_layer = mla_out
    assert CAST_STEPS == MLA_HEADS and wo.shape[1] == MLA_HEADS * V_HEAD_DIM
    args += [o_lat, wuv, wo]
    in_specs += [_tile_spec(rows, o_lat.shape[1]), _resident(wuv.shape),
                 _cast_chunk_spec(wo, wo_layer)]
    scratch.append(pltpu.VMEM(wo.shape[1:], BF16))
    blocks += 2 * _nbytes((rows, o_lat.shape[1]), BF16) + _nbytes(wuv.shape, BF16)
    chunked.append(wo)
  args += [g, wup, wdn]
  in_specs += [_layer_resident(g, layer), _cast_chunk_spec(wup, layer), _cast_chunk_spec(wdn, layer)]
  scratch += [pltpu.VMEM(wup.shape[1:], BF16), pltpu.VMEM(wdn.shape[1:], BF16)]
  blocks += sum(_nbytes(w.shape[1:], BF16) + 2 * _nbytes(w.shape[1:], F32) // CAST_STEPS
                for w in chunked)
  if final_g is not None:
    args.append(final_g)
    in_specs.append(_layer_resident(final_g, 0))
  temps = _nbytes((rows, hidden), F32) + _nbytes((rows, hidden), BF16) + 2 * _nbytes((rows, d), F32)
  return pl.pallas_call(
      functools.partial(_ffn_kernel, mla_out=mla_out is not None, final_norm=final_g is not None),
      grid=(CAST_STEPS + t // rows,),
      in_specs=in_specs,
      out_specs=_tile_spec(rows, d),
      out_shape=jax.ShapeDtypeStruct((t, d), F32),
      scratch_shapes=scratch,
      compiler_params=pltpu.CompilerParams(
          dimension_semantics=("arbitrary",),
          vmem_limit_bytes=_vmem_limit(blocks, temps)),
      name="ffn",
  )(*args)


def kernel(x, positions, norm_mix, norm_ffn, final_norm, mla_w_dkv, mla_q_norm, mla_kv_norm, mla_w_uq, mla_w_ukv, mla_w_o, sgu_w_in, sgu_ln_g, sgu_ln_b, sgu_w_spatial, sgu_b_spatial, sgu_w_out, ffn_w_up, ffn_w_down):
  batch, seq, d = x.shape
  depth = norm_mix.shape[0]
  t = batch * seq
  assert seq % MLP_ROWS == 0 and seq % ATTN_Q == 0
  xf = x.reshape(t, d)

  inv_freq = ROPE_THETA ** (-jnp.arange(0, QK_ROPE_DIM, 2, dtype=F32) / QK_ROPE_DIM)
  freq_col = inv_freq.reshape(ROPE_HALF, 1)
  pos_row = positions.astype(F32).reshape(1, t)

  vec = lambda a: a.reshape(a.shape[0], 1, a.shape[1])
  g_mix, g_ffn, g_final = vec(norm_mix), vec(norm_ffn), final_norm.reshape(1, 1, d)
  q_norm, kv_norm = vec(mla_q_norm), vec(mla_kv_norm)
  ln_g, ln_b = vec(sgu_ln_g), vec(sgu_ln_b)
  b_spatial_t = sgu_b_spatial.transpose(0, 2, 1)
  for i in range(depth):
    j = i // 2
    mla_out = None
    if i % 2 == 0:
      wdkv = jnp.pad(mla_w_dkv[j], ((0, 0), (0, V7X_LANES - QK_ROPE_DIM))).astype(BF16)
      wuq = mla_w_uq[j].reshape(Q_LORA_RANK, MLA_HEADS, QK_HEAD_DIM)
      wqnt = wuq[:, :, :QK_NOPE_DIM].reshape(Q_LORA_RANK, -1).T.astype(BF16)
      wqrt = wuq[:, :, QK_NOPE_DIM:].reshape(Q_LORA_RANK, -1).T.astype(BF16)
      wukv = mla_w_ukv[j].reshape(KV_LORA_RANK, MLA_HEADS, QK_NOPE_DIM + V_HEAD_DIM)
      wuk = wukv[:, :, :QK_NOPE_DIM].transpose(1, 0, 2).astype(BF16)
      wuv = wukv[:, :, QK_NOPE_DIM:].transpose(1, 0, 2).astype(BF16)
      q_t, kc, ct = _mla_proj(xf, (g_mix, i), wdkv, (q_norm, j), (kv_norm, j),
                              wqnt, wqrt, wuk, pos_row, freq_col)
      o_lat = _mla_attn(q_t, kc, ct, batch, seq)
      mla_out = (o_lat.reshape(t, -1), wuv, mla_w_o, j)
    else:
      xf = _sgu(xf, (g_mix, i), sgu_w_in, ln_g, ln_b, sgu_w_spatial, b_spatial_t,
                sgu_w_out, j)
    last = i == depth - 1
    xf = _ffn(xf, g_ffn, ffn_w_up, ffn_w_down, i,
              mla_out=mla_out, final_g=g_final if last else None)
  return xf.reshape(batch, seq, d)
```

```python
import functools
import math

import jax
import jax.numpy as jnp
from jax import lax
from jax.experimental import pallas as pl
from jax.experimental.pallas import tpu as pltpu

F32 = jnp.float32
BF16 = jnp.bfloat16

MLA_HEADS = 8
QK_NOPE_DIM = 128
QK_ROPE_DIM = 64
QK_HEAD_DIM = QK_NOPE_DIM + QK_ROPE_DIM
V_HEAD_DIM = 128
Q_LORA_RANK = 256
KV_LORA_RANK = 128
ROPE_THETA = 10000.0
SGU_CHUNK = 128
SGU_GROUPS = 8
NORM_EPS = 1e-6
LN_EPS = 1e-5

V7X_LANES = 128
V7X_BF16_SUBLANES = 16
V7X_VMEM_BYTES = 64 * 1024 * 1024
V7X_VMEM_RESERVE_BYTES = 8 * 1024 * 1024

QK_PAD = KV_LORA_RANK + V7X_LANES
ROPE_HALF = QK_ROPE_DIM // 2
V_ROWS = KV_LORA_RANK + V7X_BF16_SUBLANES

ATTN_Q = 512
ATTN_K = 256
PROJ_SUB_ROWS = 256
MLP_ROWS = 512
SGU_SUB_ROWS = 256
CAST_STEPS = 8


def _vmem_limit(block_bytes, temp_bytes):
  want = block_bytes + temp_bytes + V7X_VMEM_RESERVE_BYTES
  return int(min(want, V7X_VMEM_BYTES - V7X_VMEM_RESERVE_BYTES))


def _nbytes(shape, dtype):
  return math.prod(shape) * jnp.dtype(dtype).itemsize


def _dot(a, b):
  return jnp.dot(a, b, preferred_element_type=F32)


def _rms(x, g):
  return x * lax.rsqrt(jnp.mean(x * x, axis=-1, keepdims=True) + NORM_EPS) * g


def _resident(shape):
  nd = len(shape)
  return pl.BlockSpec(shape, lambda *_: (0,) * nd, pipeline_mode=pl.Buffered(1))


def _layer_resident(stacked, layer):
  nd = stacked.ndim
  return pl.BlockSpec((1,) + stacked.shape[1:], lambda *_: (layer,) + (0,) * (nd - 1),
                      pipeline_mode=pl.Buffered(1))


def _rope_t(x_t, cos_t, sin_t):
  x1, x2 = x_t[:ROPE_HALF], x_t[ROPE_HALF:]
  return jnp.concatenate([x1 * cos_t - x2 * sin_t, x2 * cos_t + x1 * sin_t], axis=0)


_FULL, _DIAG, _SKIP = "full", "diag", "skip"


def _mla_kernel(x_ref, pos_ref, freq_ref, g_ref, wdkv_ref, qn_ref, kvn_ref, wqnt_ref, wqrt_ref,
                wuk_ref, o_ref, wql_ref, q_scr, kc_scr, ct_scr, s0_ref, s1_ref, mx0_ref, mx1_ref,
                m_ref, acc_ref):
  step = pl.program_id(1)
  tiles = q_scr.shape[0]
  rows = x_ref.shape[0]
  tk = ATTN_K
  scale = QK_HEAD_DIM ** -0.5 * math.log2(math.e)

  @pl.when(jnp.logical_and(pl.program_id(0) == 0, step == 0))
  def _():
    for hd in range(MLA_HEADS):
      rws = slice(hd * QK_NOPE_DIM, (hd + 1) * QK_NOPE_DIM)
      wql_ref[rws, :] = (_dot(wuk_ref[hd], wqnt_ref[rws, :]) * scale).astype(BF16)

  def project(ti):
    subs = [slice(r0, r0 + PROJ_SUB_ROWS) for r0 in range(0, rows, PROJ_SUB_ROWS)]
    pad = jnp.zeros((V7X_LANES - QK_ROPE_DIM, PROJ_SUB_ROWS), F32)
    lats = [_dot(_rms(x_ref[sub, :], g_ref[0]).astype(BF16), wdkv_ref[...])
            for sub in subs]
    tables = []
    for sub in subs:
      ang = freq_ref[...] * pos_ref[:, sub]
      tables.append((jnp.cos(ang), jnp.sin(ang)))

    cq_ts = []
    for sub, lat, (cos_t, sin_t) in zip(subs, lats, tables):
      c_q = _rms(lat[:, :Q_LORA_RANK], qn_ref[0])
      c_kv = _rms(lat[:, Q_LORA_RANK:Q_LORA_RANK + KV_LORA_RANK], kvn_ref[0])
      kr_t = lat[:, Q_LORA_RANK + KV_LORA_RANK:].T
      kr_roped = jnp.concatenate([_rope_t(kr_t[:QK_ROPE_DIM], cos_t, sin_t), pad], axis=0).T
      key_rows = pl.ds(pl.multiple_of(ti * rows + sub.start, PROJ_SUB_ROWS), PROJ_SUB_ROWS)
      kc_scr[key_rows, :] = jnp.concatenate([c_kv.astype(BF16), kr_roped.astype(BF16)], axis=1)
      ct_scr[ti, :, sub] = jnp.concatenate(
          [c_kv.T, jnp.ones((V_ROWS - KV_LORA_RANK, PROJ_SUB_ROWS), F32)], axis=0).astype(BF16)
      cq_ts.append(c_q.T.astype(BF16))

    qs = [(_dot(wql_ref[...], cq_t), _dot(wqrt_ref[...], cq_t)) for cq_t in cq_ts]
    for sub, (ql_t, qr_t), (cos_t, sin_t) in zip(subs, qs, tables):
      for hd in range(MLA_HEADS):
        q_lat = ql_t[hd * KV_LORA_RANK:(hd + 1) * KV_LORA_RANK]
        q_rope = _rope_t(qr_t[hd * QK_ROPE_DIM:(hd + 1) * QK_ROPE_DIM], cos_t, sin_t) * scale
        q_scr[ti, :, hd * rows + sub.start:hd * rows + sub.stop] = jnp.concatenate(
            [q_lat, q_rope, pad], axis=0).astype(BF16)

  def attend(qi):
    tq = rows
    units = MLA_HEADS * tq // tk
    bufs = ((s0_ref, mx0_ref), (s1_ref, mx1_ref))
    m_ref[...] = jnp.full(m_ref.shape, -jnp.inf, F32)
    acc_ref[...] = jnp.zeros(acc_ref.shape, F32)

    def keys(kt):
      return kc_scr[pl.ds(pl.multiple_of(kt * tk, tk), tk), :]

    def score_unit(k, q_tile, buf, u):
      s_ref, mx_ref = bufs[buf]
      cols = slice(u * tk, (u + 1) * tk)
      s = _dot(k, q_scr[q_tile, :, cols])
      s_ref[:, cols] = s
      mx_ref[:, cols] = jnp.max(s, axis=0, keepdims=True)

    def softmax_unit(v_t, buf, u, mode):
      s_ref, mx_ref = bufs[buf]
      cols = slice(u * tk, (u + 1) * tk)
      s = s_ref[:, cols]
      if mode == _DIAG:
        key = lax.broadcasted_iota(jnp.int32, s.shape, 0)
        qry = lax.broadcasted_iota(jnp.int32, s.shape, 1)
        s = jnp.where(key <= qry, s, -jnp.inf)
        mx = jnp.max(s, axis=0, keepdims=True)
      else:
        mx = mx_ref[:, cols]
      m_prev = m_ref[:, cols]
      m_new = jnp.maximum(m_prev, mx)
      alpha = jnp.exp2(m_prev - m_new)
      p = jnp.exp2(s - m_new).astype(BF16)
      acc_ref[:, cols] = alpha * acc_ref[:, cols] + _dot(v_t, p)
      m_ref[:, cols] = m_new

    def stage(nxt, next_units, sb, half, buf, mode_of_unit):
      v_t = None if sb is None else ct_scr[sb, :, half * tk:(half + 1) * tk]
      for u in range(units):
        if u in next_units:
          score_unit(nxt[0], nxt[1], 1 - buf, u)
        if sb is not None and mode_of_unit(u) != _SKIP:
          softmax_unit(v_t, buf, u, mode_of_unit(u))

    all_units = range(units)

    @pl.when(qi == 0)
    def _():
      stage((keys(0), qi), all_units, None, None, 1, None)

    def body(j, carry):
      stage((keys(2 * j + 1), qi), all_units, j, 0, 0, lambda u: _FULL)
      stage((keys(2 * j + 2), qi), all_units, j, 1, 1, lambda u: _FULL)
      return carry

    lax.fori_loop(0, qi, body, 0)
    stage((keys(2 * qi + 1), qi), range(1, units, 2), qi, 0, 0,
          lambda u: _DIAG if u % 2 == 0 else _FULL)
    stage((keys(0), jnp.minimum(qi + 1, tiles - 1)), all_units, qi, 1, 1,
          lambda u: _SKIP if u % 2 == 0 else _DIAG)

    o_t = acc_ref[:KV_LORA_RANK, :] / acc_ref[KV_LORA_RANK:KV_LORA_RANK + 1, :]
    for hd in range(MLA_HEADS):
      o_ref[0, :, hd * KV_LORA_RANK:(hd + 1) * KV_LORA_RANK] = (
          o_t[:, hd * tq:(hd + 1) * tq].T.astype(o_ref.dtype))

  @pl.when(step < tiles)
  def _():
    project(step)

  @pl.when(step >= tiles)
  def _():
    attend(step - tiles)


def _mla(x, pos_row, freq_col, g, wdkv, qn, kvn, wqnt, wqrt, wuk, batch, seq):
  t, d = x.shape
  tq, tk = ATTN_Q, ATTN_K
  assert tq == 2 * tk
  tiles = seq // tq
  nq = MLA_HEADS * tq
  out_w = MLA_HEADS * KV_LORA_RANK
  token_tile = lambda b, i: b * tiles + jnp.minimum(i, tiles - 1)
  in_specs = [pl.BlockSpec((tq, d), lambda b, i: (token_tile(b, i), 0)),
              pl.BlockSpec((1, tq), lambda b, i: (0, token_tile(b, i))),
              _resident(freq_col.shape), _layer_resident(*g), _resident(wdkv.shape),
              _layer_resident(*qn), _layer_resident(*kvn), _resident(wqnt.shape),
              _resident(wqrt.shape), _resident(wuk.shape)]
  scratch_shapes = [
      pltpu.VMEM((MLA_HEADS * KV_LORA_RANK, Q_LORA_RANK), BF16),
      pltpu.VMEM((tiles, QK_PAD, nq), BF16),
      pltpu.VMEM((seq, QK_PAD), BF16),
      pltpu.VMEM((tiles, V_ROWS, tq), BF16),
      pltpu.VMEM((tk, nq), F32), pltpu.VMEM((tk, nq), F32),
      pltpu.VMEM((1, nq), F32), pltpu.VMEM((1, nq), F32),
      pltpu.VMEM((1, nq), F32), pltpu.VMEM((V_ROWS, nq), F32)]
  weights = sum(_nbytes(w.shape, w.dtype) for w in (wdkv, wqnt, wqrt, wuk))
  blocks = weights + 2 * (_nbytes((tq, d), F32) + _nbytes((tq, out_w), BF16))
  scratch = (_nbytes((tiles, QK_PAD, nq), BF16) + _nbytes((seq, QK_PAD), BF16)
             + _nbytes((tiles, V_ROWS, tq), BF16) + _nbytes((2 * tk + V_ROWS + 24, nq), F32))
  temps = 2 * _nbytes((tk, nq), F32)
  return pl.pallas_call(
      _mla_kernel,
      grid=(batch, 2 * tiles),
      in_specs=in_specs,
      out_specs=pl.BlockSpec((1, tq, out_w), lambda b, i: (b, jnp.maximum(i - tiles, 0), 0)),
      out_shape=jax.ShapeDtypeStruct((batch, seq, out_w), BF16),
      scratch_shapes=scratch_shapes,
      compiler_params=pltpu.CompilerParams(
          dimension_semantics=("arbitrary", "arbitrary"),
          vmem_limit_bytes=_vmem_limit(blocks + scratch, temps)),
      name="mla_attn",
  )(x, pos_row, freq_col, g[0], wdkv, qn[0], kvn[0], wqnt, wqrt, wuk)


def _cast_chunk_spec(stacked, layer):
  _, r, c = stacked.shape
  return pl.BlockSpec((1, r // CAST_STEPS, c),
                      lambda s: (layer, jnp.minimum(s, CAST_STEPS - 1), 0))


def _cast_chunk(step, src_ref, dst_ref):
  chunk = src_ref.shape[1]
  dst_ref[pl.ds(pl.multiple_of(step * chunk, chunk), chunk), :] = src_ref[0].astype(BF16)


def _tile_spec(rows, width):
  return pl.BlockSpec((rows, width), lambda s: (jnp.maximum(s - CAST_STEPS, 0), 0))


def _sgu_kernel(x_ref, g_ref, win_ref, lng_ref, lnb_ref, wsp_ref, bsp_ref, wout_ref,
                o_ref, win_bf, wout_bf, gated_ref):
  step = pl.program_id(0)

  @pl.when(step < CAST_STEPS)
  def _():
    _cast_chunk(step, win_ref, win_bf)
    _cast_chunk(step, wout_ref, wout_bf)

  @pl.when(step >= CAST_STEPS)
  def _():
    rows = x_ref.shape[0]
    width = lng_ref.shape[2]
    group_dim = width // SGU_GROUPS
    t_idx = lax.broadcasted_iota(jnp.int32, (SGU_CHUNK, SGU_CHUNK), 0)
    s_idx = lax.broadcasted_iota(jnp.int32, (SGU_CHUNK, SGU_CHUNK), 1)
    causal = s_idx <= t_idx
    w_sp = [jnp.where(causal, wsp_ref[0, gi], 0.0).astype(BF16)
            for gi in range(SGU_GROUPS)]

    subs = [slice(r0, r0 + SGU_SUB_ROWS) for r0 in range(0, rows, SGU_SUB_ROWS)]
    zs = [_dot(_rms(x_ref[sub, :], g_ref[0]).astype(BF16), win_bf[...]) for sub in subs]
    for sub, z in zip(subs, zs):
      r0 = sub.start
      z = 0.5 * z * (1.0 + lax.erf(z * (0.5 ** 0.5)))
      u = z[:, :width]
      v = z[:, width:]
      mu = jnp.mean(v, axis=-1, keepdims=True)
      var = jnp.mean(jnp.square(v - mu), axis=-1, keepdims=True)
      v = ((v - mu) * lax.rsqrt(var + LN_EPS) * lng_ref[0] + lnb_ref[0]).astype(BF16)
      for gi in range(SGU_GROUPS):
        bias = jnp.broadcast_to(bsp_ref[0, :, gi:gi + 1], (SGU_CHUNK, group_dim))
        cols = slice(gi * group_dim, (gi + 1) * group_dim)
        for c0 in range(0, SGU_SUB_ROWS, SGU_CHUNK):
          rws = slice(c0, c0 + SGU_CHUNK)
          mixed = _dot(w_sp[gi], v[rws, cols]) + bias
          gated_ref[r0 + c0:r0 + c0 + SGU_CHUNK, cols] = (u[rws, cols] * mixed).astype(BF16)
      o_ref[sub, :] = x_ref[sub, :] + _dot(gated_ref[sub, :], wout_bf[...])


def _sgu(x, g, win, lng, lnb, wsp, bsp_t, wout, layer):
  t, d = x.shape
  rows = MLP_ROWS
  width = lng.shape[2]
  scratch = [pltpu.VMEM(win.shape[1:], BF16), pltpu.VMEM(wout.shape[1:], BF16),
             pltpu.VMEM((rows, width), BF16)]
  blocks = (2 * (_nbytes(win.shape[1:], F32) + _nbytes(wout.shape[1:], F32)) // CAST_STEPS
            + _nbytes(win.shape[1:], BF16) + _nbytes(wout.shape[1:], BF16)
            + _nbytes(wsp.shape[1:], F32)
            + 4 * _nbytes((rows, d), F32) + _nbytes((rows, width), BF16))
  temps = 3 * _nbytes((rows, 2 * width), F32)
  return pl.pallas_call(
      _sgu_kernel,
      grid=(CAST_STEPS + t // rows,),
      in_specs=[_tile_spec(rows, d), _layer_resident(*g),
                _cast_chunk_spec(win, layer), _layer_resident(lng, layer),
                _layer_resident(lnb, layer), _layer_resident(wsp, layer),
                _layer_resident(bsp_t, layer), _cast_chunk_spec(wout, layer)],
      out_specs=_tile_spec(rows, d),
      out_shape=jax.ShapeDtypeStruct((t, d), F32),
      scratch_shapes=scratch,
      compiler_params=pltpu.CompilerParams(
          dimension_semantics=("arbitrary",),
          vmem_limit_bytes=_vmem_limit(blocks, temps)),
      name="sgu_mixer",
  )(x, g[0], win, lng, lnb, wsp, bsp_t, wout)


def _ffn_kernel(*refs, mla_out, final_norm):
  refs = list(refs)
  x_ref = refs.pop(0)
  if mla_out:
    ol_ref, wuv_ref, wo_ref = refs[:3]
    del refs[:3]
  g_ref, wup_ref, wdn_ref = refs[:3]
  del refs[:3]
  gf_ref = refs.pop(0) if final_norm else None
  o_ref = refs.pop(0)
  wo_bf = refs.pop(0) if mla_out else None
  wup_bf, wdn_bf = refs
  step = pl.program_id(0)

  @pl.when(step < CAST_STEPS)
  def _():
    _cast_chunk(step, wup_ref, wup_bf)
    _cast_chunk(step, wdn_ref, wdn_bf)
    if mla_out:
      rws = pl.ds(pl.multiple_of(step * V_HEAD_DIM, V_HEAD_DIM), V_HEAD_DIM)
      wo_bf[rws, :] = _dot(wuv_ref[step], wo_ref[0].astype(BF16)).astype(BF16)

  @pl.when(step >= CAST_STEPS)
  def _():
    x = x_ref[...]
    if mla_out:
      x = x + _dot(ol_ref[...], wo_bf[...])
    h = _rms(x, g_ref[0]).astype(BF16)
    a = jnp.square(jnp.maximum(_dot(h, wup_bf[...]), 0.0)).astype(BF16)
    y = x + _dot(a, wdn_bf[...])
    if final_norm:
      y = _rms(y, gf_ref[0])
    o_ref[...] = y


def _ffn(x, g, wup, wdn, layer, mla_out=None, final_g=None):
  t, d = x.shape
  rows = MLP_ROWS
  hidden = wup.shape[2]
  args = [x]
  in_specs = [_tile_spec(rows, d)]
  scratch = []
  blocks = 4 * _nbytes((rows, d), F32)
  chunked = [wup, wdn]
  if mla_out is not None:
    o_lat, wuv, wo, wo_layer = mla_out
    assert CAST_STEPS == MLA_HEADS and wo.shape[1] == MLA_HEADS * V_HEAD_DIM
    args += [o_lat, wuv, wo]
    in_specs += [_tile_spec(rows, o_lat.shape[1]), _resident(wuv.shape),
                 _cast_chunk_spec(wo, wo_layer)]
    scratch.append(pltpu.VMEM(wo.shape[1:], BF16))
    blocks += 2 * _nbytes((rows, o_lat.shape[1]), BF16) + _nbytes(wuv.shape, BF16)
    chunked.append(wo)
  args += [g, wup, wdn]
  in_specs += [_layer_resident(g, layer), _cast_chunk_spec(wup, layer), _cast_chunk_spec(wdn, layer)]
  scratch += [pltpu.VMEM(wup.shape[1:], BF16), pltpu.VMEM(wdn.shape[1:], BF16)]
  blocks += sum(_nbytes(w.shape[1:], BF16) + 2 * _nbytes(w.shape[1:], F32) // CAST_STEPS
                for w in chunked)
  if final_g is not None:
    args.append(final_g)
    in_specs.append(_layer_resident(final_g, 0))
  temps = _nbytes((rows, hidden), F32) + _nbytes((rows, hidden), BF16) + 2 * _nbytes((rows, d), F32)
  return pl.pallas_call(
      functools.partial(_ffn_kernel, mla_out=mla_out is not None, final_norm=final_g is not None),
      grid=(CAST_STEPS + t // rows,),
      in_specs=in_specs,
      out_specs=_tile_spec(rows, d),
      out_shape=jax.ShapeDtypeStruct((t, d), F32),
      scratch_shapes=scratch,
      compiler_params=pltpu.CompilerParams(
          dimension_semantics=("arbitrary",),
          vmem_limit_bytes=_vmem_limit(blocks, temps)),
      name="ffn",
  )(*args)


def kernel(x, positions, norm_mix, norm_ffn, final_norm, mla_w_dkv, mla_q_norm, mla_kv_norm, mla_w_uq, mla_w_ukv, mla_w_o, sgu_w_in, sgu_ln_g, sgu_ln_b, sgu_w_spatial, sgu_b_spatial, sgu_w_out, ffn_w_up, ffn_w_down):
  batch, seq, d = x.shape
  depth = norm_mix.shape[0]
  t = batch * seq
  assert seq % MLP_ROWS == 0 and seq % ATTN_Q == 0
  xf = x.reshape(t, d)

  inv_freq = ROPE_THETA ** (-jnp.arange(0, QK_ROPE_DIM, 2, dtype=F32) / QK_ROPE_DIM)
  freq_col = inv_freq.reshape(ROPE_HALF, 1)
  pos_row = positions.astype(F32).reshape(1, t)

  vec = lambda a: a.reshape(a.shape[0], 1, a.shape[1])
  g_mix, g_ffn, g_final = vec(norm_mix), vec(norm_ffn), final_norm.reshape(1, 1, d)
  q_norm, kv_norm = vec(mla_q_norm), vec(mla_kv_norm)
  ln_g, ln_b = vec(sgu_ln_g), vec(sgu_ln_b)
  b_spatial_t = sgu_b_spatial.transpose(0, 2, 1)
  for i in range(depth):
    j = i // 2
    mla_out = None
    if i % 2 == 0:
      wdkv = jnp.pad(mla_w_dkv[j], ((0, 0), (0, V7X_LANES - QK_ROPE_DIM))).astype(BF16)
      wuq = mla_w_uq[j].reshape(Q_LORA_RANK, MLA_HEADS, QK_HEAD_DIM)
      wqnt = wuq[:, :, :QK_NOPE_DIM].reshape(Q_LORA_RANK, -1).T.astype(BF16)
      wqrt = wuq[:, :, QK_NOPE_DIM:].reshape(Q_LORA_RANK, -1).T.astype(BF16)
      wukv = mla_w_ukv[j].reshape(KV_LORA_RANK, MLA_HEADS, QK_NOPE_DIM + V_HEAD_DIM)
      wuk = wukv[:, :, :QK_NOPE_DIM].transpose(1, 0, 2).astype(BF16)
      wuv = wukv[:, :, QK_NOPE_DIM:].transpose(1, 0, 2).astype(BF16)
      o_lat = _mla(xf, pos_row, freq_col, (g_mix, i), wdkv, (q_norm, j), (kv_norm, j),
                   wqnt, wqrt, wuk, batch, seq)
      mla_out = (o_lat.reshape(t, -1), wuv, mla_w_o, j)
    else:
      xf = _sgu(xf, (g_mix, i), sgu_w_in, ln_g, ln_b, sgu_w_spatial, b_spatial_t,
                sgu_w_out, j)
    last = i == depth - 1
    xf = _ffn(xf, g_ffn, ffn_w_up, ffn_w_down, i,
              mla_out=mla_out, final_g=g_final if last else None)
  return xf.reshape(batch, seq, d)
```

```python
import functools
import math

import jax
import jax.numpy as jnp
from jax import lax
from jax.experimental import pallas as pl
from jax.experimental.pallas import tpu as pltpu

F32 = jnp.float32
BF16 = jnp.bfloat16

MLA_HEADS = 8
QK_NOPE_DIM = 128
QK_ROPE_DIM = 64
QK_HEAD_DIM = QK_NOPE_DIM + QK_ROPE_DIM
V_HEAD_DIM = 128
Q_LORA_RANK = 256
KV_LORA_RANK = 128
ROPE_THETA = 10000.0
SGU_CHUNK = 128
SGU_GROUPS = 8
NORM_EPS = 1e-6
LN_EPS = 1e-5

V7X_LANES = 128
V7X_BF16_SUBLANES = 16
V7X_VMEM_BYTES = 64 * 1024 * 1024
V7X_VMEM_RESERVE_BYTES = 8 * 1024 * 1024

QK_PAD = KV_LORA_RANK + V7X_LANES
ROPE_HALF = QK_ROPE_DIM // 2
V_ROWS = KV_LORA_RANK + V7X_BF16_SUBLANES

ATTN_Q = 512
ATTN_K = 256
MLP_ROWS = 512
FFN_PLAIN_ROWS = 1024
FFN_HIDDEN_CHUNK = 1024
SGU_SUB_ROWS = 256
CAST_STEPS = 8


def _vmem_limit(block_bytes, temp_bytes):
  want = block_bytes + temp_bytes + V7X_VMEM_RESERVE_BYTES
  return int(min(want, V7X_VMEM_BYTES - V7X_VMEM_RESERVE_BYTES))


def _nbytes(shape, dtype):
  return math.prod(shape) * jnp.dtype(dtype).itemsize


def _dot(a, b):
  return jnp.dot(a, b, preferred_element_type=F32)


def _rms(x, g):
  return x * lax.rsqrt(jnp.mean(x * x, axis=-1, keepdims=True) + NORM_EPS) * g


def _resident(shape):
  nd = len(shape)
  return pl.BlockSpec(shape, lambda *_: (0,) * nd, pipeline_mode=pl.Buffered(1))


def _layer_resident(stacked, layer):
  nd = stacked.ndim
  return pl.BlockSpec((1,) + stacked.shape[1:], lambda *_: (layer,) + (0,) * (nd - 1),
                      pipeline_mode=pl.Buffered(1))


def _rope_t(x_t, cos_t, sin_t):
  x1, x2 = x_t[:ROPE_HALF], x_t[ROPE_HALF:]
  return jnp.concatenate([x1 * cos_t - x2 * sin_t, x2 * cos_t + x1 * sin_t], axis=0)


def _mla_proj_kernel(x_ref, g_ref, wdkv_ref, qn_ref, kvn_ref, wqnt_ref, wqrt_ref,
                     wuk_ref, pos_ref, freq_ref, qt_ref, kc_ref, ct_ref, wql_ref):
  rows = x_ref.shape[0]
  scale = QK_HEAD_DIM ** -0.5 * math.log2(math.e)

  @pl.when(pl.program_id(0) == 0)
  def _():
    for hd in range(MLA_HEADS):
      rws = slice(hd * QK_NOPE_DIM, (hd + 1) * QK_NOPE_DIM)
      wql_ref[rws, :] = (_dot(wuk_ref[hd], wqnt_ref[rws, :]) * scale).astype(BF16)

  h = _rms(x_ref[...], g_ref[0]).astype(BF16)
  lat = _dot(h, wdkv_ref[...])
  c_q = _rms(lat[:, :Q_LORA_RANK], qn_ref[0])
  c_kv = _rms(lat[:, Q_LORA_RANK:Q_LORA_RANK + KV_LORA_RANK], kvn_ref[0])
  ang = freq_ref[...] * pos_ref[...]
  cos_t = jnp.cos(ang)
  sin_t = jnp.sin(ang)
  pad = jnp.zeros((V7X_LANES - QK_ROPE_DIM, rows), F32)

  kr_t = lat[:, Q_LORA_RANK + KV_LORA_RANK:].T
  kr_roped = jnp.concatenate([_rope_t(kr_t[:QK_ROPE_DIM], cos_t, sin_t), pad], axis=0).T
  kc_ref[...] = jnp.concatenate([c_kv.astype(BF16), kr_roped.astype(BF16)], axis=1)
  ct_ref[0] = jnp.concatenate(
      [c_kv.T, jnp.ones((V_ROWS - KV_LORA_RANK, rows), F32)], axis=0).astype(BF16)

  cq_t = c_q.T.astype(BF16)
  ql_t = _dot(wql_ref[...], cq_t)
  qr_t = _dot(wqrt_ref[...], cq_t)
  for hd in range(MLA_HEADS):
    q_lat = ql_t[hd * KV_LORA_RANK:(hd + 1) * KV_LORA_RANK]
    q_rope = _rope_t(qr_t[hd * QK_ROPE_DIM:(hd + 1) * QK_ROPE_DIM], cos_t, sin_t) * scale
    qt_ref[0, :, hd * rows:(hd + 1) * rows] = jnp.concatenate(
        [q_lat, q_rope, pad], axis=0).astype(BF16)


def _mla_proj(x, g, wdkv, qn, kvn, wqnt, wqrt, wuk, pos_row, freq_col):
  t, d = x.shape
  rows = ATTN_Q
  steps = t // rows
  row_spec = lambda width: pl.BlockSpec((rows, width), lambda i: (i, 0))
  in_specs = [row_spec(d), _layer_resident(*g), _resident(wdkv.shape), _layer_resident(*qn),
              _layer_resident(*kvn), _resident(wqnt.shape), _resident(wqrt.shape),
              _resident(wuk.shape), pl.BlockSpec((1, rows), lambda i: (0, i)),
              _resident(freq_col.shape)]
  out_shape = [
      jax.ShapeDtypeStruct((steps, QK_PAD, MLA_HEADS * rows), BF16),
      jax.ShapeDtypeStruct((t, QK_PAD), BF16),
      jax.ShapeDtypeStruct((steps, V_ROWS, rows), BF16),
  ]
  out_specs = [
      pl.BlockSpec((1, QK_PAD, MLA_HEADS * rows), lambda i: (i, 0, 0)),
      row_spec(QK_PAD),
      pl.BlockSpec((1, V_ROWS, rows), lambda i: (i, 0, 0)),
  ]
  weights = sum(_nbytes(w.shape, w.dtype) for w in (wdkv, wqnt, wqrt, wuk))
  blocks = weights + 2 * (_nbytes((rows, d), F32)
                          + _nbytes((MLA_HEADS + 2, rows, QK_PAD), BF16))
  temps = 6 * _nbytes((rows, MLA_HEADS * QK_HEAD_DIM), F32)
  return pl.pallas_call(
      _mla_proj_kernel,
      grid=(steps,),
      in_specs=in_specs,
      out_specs=out_specs,
      out_shape=out_shape,
      scratch_shapes=[pltpu.VMEM((MLA_HEADS * KV_LORA_RANK, Q_LORA_RANK), BF16)],
      compiler_params=pltpu.CompilerParams(
          dimension_semantics=("arbitrary",),
          vmem_limit_bytes=_vmem_limit(blocks, temps)),
      name="mla_proj",
  )(x, g[0], wdkv, qn[0], kvn[0], wqnt, wqrt, wuk, pos_row, freq_col)


_FULL, _DIAG, _SKIP = "full", "diag", "skip"


def _mla_attn_kernel(qt_ref, kc_ref, ct_ref, qn_ref, kn_ref, o_ref, s0_ref, s1_ref,
                     mx0_ref, mx1_ref, m_ref, acc_ref):
  qi = pl.program_id(1)
  tk = ATTN_K
  tq = qt_ref.shape[3] // MLA_HEADS
  units = qt_ref.shape[3] // tk
  bufs = ((s0_ref, mx0_ref), (s1_ref, mx1_ref))

  m_ref[...] = jnp.full(m_ref.shape, -jnp.inf, F32)
  acc_ref[...] = jnp.zeros(acc_ref.shape, F32)

  def keys(kt):
    return kc_ref[0, pl.ds(pl.multiple_of(kt * tk, tk), tk), :]

  def score_unit(k, q_ref, buf, u):
    s_ref, mx_ref = bufs[buf]
    cols = slice(u * tk, (u + 1) * tk)
    s = _dot(k, q_ref[0, 0, :, cols])
    s_ref[:, cols] = s
    mx_ref[:, cols] = jnp.max(s, axis=0, keepdims=True)

  def softmax_unit(v_t, buf, u, mode):
    s_ref, mx_ref = bufs[buf]
    cols = slice(u * tk, (u + 1) * tk)
    s = s_ref[:, cols]
    if mode == _DIAG:
      key = lax.broadcasted_iota(jnp.int32, s.shape, 0)
      qry = lax.broadcasted_iota(jnp.int32, s.shape, 1)
      s = jnp.where(key <= qry, s, -jnp.inf)
      mx = jnp.max(s, axis=0, keepdims=True)
    else:
      mx = mx_ref[:, cols]
    m_prev = m_ref[:, cols]
    m_new = jnp.maximum(m_prev, mx)
    alpha = jnp.exp2(m_prev - m_new)
    p = jnp.exp2(s - m_new).astype(BF16)
    acc_ref[:, cols] = alpha * acc_ref[:, cols] + _dot(v_t, p)
    m_ref[:, cols] = m_new

  def step(nxt, next_units, sb, half, buf, mode_of_unit):
    v_t = None if sb is None else ct_ref[0, sb, :, half * tk:(half + 1) * tk]
    for u in range(units):
      if u in next_units:
        score_unit(nxt[0], nxt[1], 1 - buf, u)
      if sb is not None and mode_of_unit(u) != _SKIP:
        softmax_unit(v_t, buf, u, mode_of_unit(u))

  all_units = range(units)

  @pl.when(jnp.logical_and(pl.program_id(0) == 0, qi == 0))
  def _():
    step((keys(0), qt_ref), all_units, None, None, 1, None)

  def body(j, carry):
    step((keys(2 * j + 1), qt_ref), all_units, j, 0, 0, lambda u: _FULL)
    step((keys(2 * j + 2), qt_ref), all_units, j, 1, 1, lambda u: _FULL)
    return carry

  lax.fori_loop(0, qi, body, 0)
  step((keys(2 * qi + 1), qt_ref), range(1, units, 2), qi, 0, 0,
       lambda u: _DIAG if u % 2 == 0 else _FULL)
  step((kn_ref[0], qn_ref), all_units, qi, 1, 1, lambda u: _SKIP if u % 2 == 0 else _DIAG)

  o_t = acc_ref[:KV_LORA_RANK, :] / acc_ref[KV_LORA_RANK:KV_LORA_RANK + 1, :]
  for hd in range(MLA_HEADS):
    o_ref[0, :, hd * KV_LORA_RANK:(hd + 1) * KV_LORA_RANK] = (
        o_t[:, hd * tq:(hd + 1) * tq].T.astype(o_ref.dtype))


def _mla_attn(q_t, kc, ct, batch, seq):
  tq, tk = ATTN_Q, ATTN_K
  assert tq == 2 * tk
  nq = MLA_HEADS * tq
  q_tiles = seq // tq
  qt4 = q_t.reshape(batch, q_tiles, QK_PAD, nq)
  kc3 = kc.reshape(batch, seq, QK_PAD)
  ct4 = ct.reshape(batch, q_tiles, V_ROWS, tq)
  out_w = MLA_HEADS * KV_LORA_RANK
  blocks = 2 * (2 * _nbytes((QK_PAD, nq), BF16) + _nbytes((seq + tk, QK_PAD), BF16)
                + _nbytes((V_ROWS, seq), BF16) + _nbytes((tq, out_w), BF16))

  def next_tile(b, i):
    flat = jnp.minimum(b * q_tiles + i + 1, batch * q_tiles - 1)
    return flat // q_tiles, flat % q_tiles

  scratch = _nbytes((2 * tk + V_ROWS + 24, nq), F32)
  temps = 2 * _nbytes((tk, nq), F32)
  return pl.pallas_call(
      _mla_attn_kernel,
      grid=(batch, q_tiles),
      in_specs=[
          pl.BlockSpec((1, 1, QK_PAD, nq), lambda b, i: (b, i, 0, 0)),
          pl.BlockSpec((1, seq, QK_PAD), lambda b, i: (b, 0, 0)),
          pl.BlockSpec((1, q_tiles, V_ROWS, tq), lambda b, i: (b, 0, 0, 0)),
          pl.BlockSpec((1, 1, QK_PAD, nq), lambda b, i: (*next_tile(b, i), 0, 0)),
          pl.BlockSpec((1, tk, QK_PAD), lambda b, i: (next_tile(b, i)[0], 0, 0)),
      ],
      out_specs=pl.BlockSpec((1, tq, out_w), lambda b, i: (b, i, 0)),
      out_shape=jax.ShapeDtypeStruct((batch, seq, out_w), BF16),
      scratch_shapes=[pltpu.VMEM((tk, nq), F32), pltpu.VMEM((tk, nq), F32),
                      pltpu.VMEM((1, nq), F32), pltpu.VMEM((1, nq), F32),
                      pltpu.VMEM((1, nq), F32), pltpu.VMEM((V_ROWS, nq), F32)],
      compiler_params=pltpu.CompilerParams(
          dimension_semantics=("arbitrary", "arbitrary"),
          vmem_limit_bytes=_vmem_limit(blocks + scratch, temps)),
      name="mla_attn",
  )(qt4, kc3, ct4, qt4, kc3)


def _cast_chunk_spec(stacked, layer):
  _, r, c = stacked.shape
  return pl.BlockSpec((1, r // CAST_STEPS, c),
                      lambda s: (layer, jnp.minimum(s, CAST_STEPS - 1), 0))


def _cast_chunk(step, src_ref, dst_ref):
  chunk = src_ref.shape[1]
  dst_ref[pl.ds(pl.multiple_of(step * chunk, chunk), chunk), :] = src_ref[0].astype(BF16)


def _tile_spec(rows, width):
  return pl.BlockSpec((rows, width), lambda s: (jnp.maximum(s - CAST_STEPS, 0), 0))


def _sgu_kernel(x_ref, g_ref, win_ref, lng_ref, lnb_ref, wsp_ref, bsp_ref, wout_ref,
                o_ref, win_bf, wout_bf, gated_ref):
  step = pl.program_id(0)

  @pl.when(step < CAST_STEPS)
  def _():
    _cast_chunk(step, win_ref, win_bf)
    _cast_chunk(step, wout_ref, wout_bf)

  @pl.when(step >= CAST_STEPS)
  def _():
    rows = x_ref.shape[0]
    width = lng_ref.shape[2]
    group_dim = width // SGU_GROUPS
    t_idx = lax.broadcasted_iota(jnp.int32, (SGU_CHUNK, SGU_CHUNK), 0)
    s_idx = lax.broadcasted_iota(jnp.int32, (SGU_CHUNK, SGU_CHUNK), 1)
    causal = s_idx <= t_idx
    w_sp = [jnp.where(causal, wsp_ref[0, gi], 0.0).astype(BF16)
            for gi in range(SGU_GROUPS)]

    subs = [slice(r0, r0 + SGU_SUB_ROWS) for r0 in range(0, rows, SGU_SUB_ROWS)]
    zs = [_dot(_rms(x_ref[sub, :], g_ref[0]).astype(BF16), win_bf[...]) for sub in subs]
    for sub, z in zip(subs, zs):
      r0 = sub.start
      z = 0.5 * z * (1.0 + lax.erf(z * (0.5 ** 0.5)))
      u = z[:, :width]
      v = z[:, width:]
      mu = jnp.mean(v, axis=-1, keepdims=True)
      var = jnp.mean(jnp.square(v - mu), axis=-1, keepdims=True)
      v = ((v - mu) * lax.rsqrt(var + LN_EPS) * lng_ref[0] + lnb_ref[0]).astype(BF16)
      for gi in range(SGU_GROUPS):
        bias = jnp.broadcast_to(bsp_ref[0, :, gi:gi + 1], (SGU_CHUNK, group_dim))
        cols = slice(gi * group_dim, (gi + 1) * group_dim)
        for c0 in range(0, SGU_SUB_ROWS, SGU_CHUNK):
          rws = slice(c0, c0 + SGU_CHUNK)
          mixed = _dot(w_sp[gi], v[rws, cols]) + bias
          gated_ref[r0 + c0:r0 + c0 + SGU_CHUNK, cols] = (u[rws, cols] * mixed).astype(BF16)
      o_ref[sub, :] = x_ref[sub, :] + _dot(gated_ref[sub, :], wout_bf[...])


def _sgu(x, g, win, lng, lnb, wsp, bsp_t, wout, layer):
  t, d = x.shape
  rows = MLP_ROWS
  width = lng.shape[2]
  scratch = [pltpu.VMEM(win.shape[1:], BF16), pltpu.VMEM(wout.shape[1:], BF16),
             pltpu.VMEM((rows, width), BF16)]
  blocks = (2 * (_nbytes(win.shape[1:], F32) + _nbytes(wout.shape[1:], F32)) // CAST_STEPS
            + _nbytes(win.shape[1:], BF16) + _nbytes(wout.shape[1:], BF16)
            + _nbytes(wsp.shape[1:], F32)
            + 4 * _nbytes((rows, d), F32) + _nbytes((rows, width), BF16))
  temps = 3 * _nbytes((rows, 2 * width), F32)
  return pl.pallas_call(
      _sgu_kernel,
      grid=(CAST_STEPS + t // rows,),
      in_specs=[_tile_spec(rows, d), _layer_resident(*g),
                _cast_chunk_spec(win, layer), _layer_resident(lng, layer),
                _layer_resident(lnb, layer), _layer_resident(wsp, layer),
                _layer_resident(bsp_t, layer), _cast_chunk_spec(wout, layer)],
      out_specs=_tile_spec(rows, d),
      out_shape=jax.ShapeDtypeStruct((t, d), F32),
      scratch_shapes=scratch,
      compiler_params=pltpu.CompilerParams(
          dimension_semantics=("arbitrary",),
          vmem_limit_bytes=_vmem_limit(blocks, temps)),
      name="sgu_mixer",
  )(x, g[0], win, lng, lnb, wsp, bsp_t, wout)


def _ffn_kernel(*refs, mla_out, final_norm):
  refs = list(refs)
  x_ref = refs.pop(0)
  if mla_out:
    ol_ref, wuv_ref, wo_ref = refs[:3]
    del refs[:3]
  g_ref, wup_ref, wdn_ref = refs[:3]
  del refs[:3]
  gf_ref = refs.pop(0) if final_norm else None
  o_ref = refs.pop(0)
  wo_bf = refs.pop(0) if mla_out else None
  wup_bf, wdn_bf = refs
  step = pl.program_id(0)

  @pl.when(step < CAST_STEPS)
  def _():
    _cast_chunk(step, wup_ref, wup_bf)
    _cast_chunk(step, wdn_ref, wdn_bf)
    if mla_out:
      rws = pl.ds(pl.multiple_of(step * V_HEAD_DIM, V_HEAD_DIM), V_HEAD_DIM)
      wo_bf[rws, :] = _dot(wuv_ref[step], wo_ref[0].astype(BF16)).astype(BF16)

  @pl.when(step >= CAST_STEPS)
  def _():
    x = x_ref[...]
    if mla_out:
      x = x + _dot(ol_ref[...], wo_bf[...])
    h = _rms(x, g_ref[0]).astype(BF16)
    y = x
    hidden = wup_bf.shape[1]
    for c0 in range(0, hidden, FFN_HIDDEN_CHUNK if x.shape[0] > MLP_ROWS else hidden):
      c1 = c0 + (FFN_HIDDEN_CHUNK if x.shape[0] > MLP_ROWS else hidden)
      a = jnp.square(jnp.maximum(_dot(h, wup_bf[:, c0:c1]), 0.0)).astype(BF16)
      y = y + _dot(a, wdn_bf[c0:c1, :])
    if final_norm:
      y = _rms(y, gf_ref[0])
    o_ref[...] = y


def _ffn(x, g, wup, wdn, layer, mla_out=None, final_g=None):
  t, d = x.shape
  rows = MLP_ROWS if mla_out is not None else FFN_PLAIN_ROWS
  hidden = wup.shape[2]
  args = [x]
  in_specs = [_tile_spec(rows, d)]
  scratch = []
  blocks = 4 * _nbytes((rows, d), F32)
  chunked = [wup, wdn]
  if mla_out is not None:
    o_lat, wuv, wo, wo_layer = mla_out
    assert CAST_STEPS == MLA_HEADS and wo.shape[1] == MLA_HEADS * V_HEAD_DIM
    args += [o_lat, wuv, wo]
    in_specs += [_tile_spec(rows, o_lat.shape[1]), _resident(wuv.shape),
                 _cast_chunk_spec(wo, wo_layer)]
    scratch.append(pltpu.VMEM(wo.shape[1:], BF16))
    blocks += 2 * _nbytes((rows, o_lat.shape[1]), BF16) + _nbytes(wuv.shape, BF16)
    chunked.append(wo)
  args += [g, wup, wdn]
  in_specs += [_layer_resident(g, layer), _cast_chunk_spec(wup, layer), _cast_chunk_spec(wdn, layer)]
  scratch += [pltpu.VMEM(wup.shape[1:], BF16), pltpu.VMEM(wdn.shape[1:], BF16)]
  blocks += sum(_nbytes(w.shape[1:], BF16) + 2 * _nbytes(w.shape[1:], F32) // CAST_STEPS
                for w in chunked)
  if final_g is not None:
    args.append(final_g)
    in_specs.append(_layer_resident(final_g, 0))
  chunk = hidden if rows <= MLP_ROWS else FFN_HIDDEN_CHUNK
  temps = _nbytes((rows, chunk), F32) + _nbytes((rows, chunk), BF16) + 3 * _nbytes((rows, d), F32)
  return pl.pallas_call(
      functools.partial(_ffn_kernel, mla_out=mla_out is not None, final_norm=final_g is not None),
      grid=(CAST_STEPS + t // rows,),
      in_specs=in_specs,
      out_specs=_tile_spec(rows, d),
      out_shape=jax.ShapeDtypeStruct((t, d), F32),
      scratch_shapes=scratch,
      compiler_params=pltpu.CompilerParams(
          dimension_semantics=("arbitrary",),
          vmem_limit_bytes=_vmem_limit(blocks, temps)),
      name="ffn",
  )(*args)


def kernel(x, positions, norm_mix, norm_ffn, final_norm, mla_w_dkv, mla_q_norm, mla_kv_norm, mla_w_uq, mla_w_ukv, mla_w_o, sgu_w_in, sgu_ln_g, sgu_ln_b, sgu_w_spatial, sgu_b_spatial, sgu_w_out, ffn_w_up, ffn_w_down):
  batch, seq, d = x.shape
  depth = norm_mix.shape[0]
  t = batch * seq
  assert seq % MLP_ROWS == 0 and seq % ATTN_Q == 0
  xf = x.reshape(t, d)

  inv_freq = ROPE_THETA ** (-jnp.arange(0, QK_ROPE_DIM, 2, dtype=F32) / QK_ROPE_DIM)
  freq_col = inv_freq.reshape(ROPE_HALF, 1)
  pos_row = positions.astype(F32).reshape(1, t)

  vec = lambda a: a.reshape(a.shape[0], 1, a.shape[1])
  g_mix, g_ffn, g_final = vec(norm_mix), vec(norm_ffn), final_norm.reshape(1, 1, d)
  q_norm, kv_norm = vec(mla_q_norm), vec(mla_kv_norm)
  ln_g, ln_b = vec(sgu_ln_g), vec(sgu_ln_b)
  b_spatial_t = sgu_b_spatial.transpose(0, 2, 1)
  for i in range(depth):
    j = i // 2
    mla_out = None
    if i % 2 == 0:
      wdkv = jnp.pad(mla_w_dkv[j], ((0, 0), (0, V7X_LANES - QK_ROPE_DIM))).astype(BF16)
      wuq = mla_w_uq[j].reshape(Q_LORA_RANK, MLA_HEADS, QK_HEAD_DIM)
      wqnt = wuq[:, :, :QK_NOPE_DIM].reshape(Q_LORA_RANK, -1).T.astype(BF16)
      wqrt = wuq[:, :, QK_NOPE_DIM:].reshape(Q_LORA_RANK, -1).T.astype(BF16)
      wukv = mla_w_ukv[j].reshape(KV_LORA_RANK, MLA_HEADS, QK_NOPE_DIM + V_HEAD_DIM)
      wuk = wukv[:, :, :QK_NOPE_DIM].transpose(1, 0, 2).astype(BF16)
      wuv = wukv[:, :, QK_NOPE_DIM:].transpose(1, 0, 2).astype(BF16)
      q_t, kc, ct = _mla_proj(xf, (g_mix, i), wdkv, (q_norm, j), (kv_norm, j),
                              wqnt, wqrt, wuk, pos_row, freq_col)
      o_lat = _mla_attn(q_t, kc, ct, batch, seq)
      mla_out = (o_lat.reshape(t, -1), wuv, mla_w_o, j)
    else:
      xf = _sgu(xf, (g_mix, i), sgu_w_in, ln_g, ln_b, sgu_w_spatial, b_spatial_t,
                sgu_w_out, j)
    last = i == depth - 1
    xf = _ffn(xf, g_ffn, ffn_w_up, ffn_w_down, i,
              mla_out=mla_out, final_g=g_final if last else None)
  return xf.reshape(batch, seq, d)
```

```python
import functools
import math

import jax
import jax.numpy as jnp
from jax import lax
from jax.experimental import pallas as pl
from jax.experimental.pallas import tpu as pltpu

F32 = jnp.float32
BF16 = jnp.bfloat16

MLA_HEADS = 8
QK_NOPE_DIM = 128
QK_ROPE_DIM = 64
QK_HEAD_DIM = QK_NOPE_DIM + QK_ROPE_DIM
V_HEAD_DIM = 128
Q_LORA_RANK = 256
KV_LORA_RANK = 128
ROPE_THETA = 10000.0
SGU_CHUNK = 128
SGU_GROUPS = 8
NORM_EPS = 1e-6
LN_EPS = 1e-5

V7X_LANES = 128
V7X_BF16_SUBLANES = 16
V7X_VMEM_BYTES = 64 * 1024 * 1024
V7X_VMEM_RESERVE_BYTES = 8 * 1024 * 1024

QK_PAD = KV_LORA_RANK + V7X_LANES
ROPE_HALF = QK_ROPE_DIM // 2
V_ROWS = KV_LORA_RANK + V7X_BF16_SUBLANES

ATTN_Q = 512
ATTN_K = 256
PROJ_TILES_PER_STEP = 2
MLP_ROWS = 512
FFN_PLAIN_ROWS = 1024
FFN_HIDDEN_CHUNK = 1024
SGU_SUB_ROWS = 256
CAST_STEPS = 8


def _vmem_limit(block_bytes, temp_bytes):
  want = block_bytes + temp_bytes + V7X_VMEM_RESERVE_BYTES
  return int(min(want, V7X_VMEM_BYTES - V7X_VMEM_RESERVE_BYTES))


def _nbytes(shape, dtype):
  return math.prod(shape) * jnp.dtype(dtype).itemsize


def _dot(a, b):
  return jnp.dot(a, b, preferred_element_type=F32)


def _rms(x, g):
  return x * lax.rsqrt(jnp.mean(x * x, axis=-1, keepdims=True) + NORM_EPS) * g


def _resident(shape):
  nd = len(shape)
  return pl.BlockSpec(shape, lambda *_: (0,) * nd, pipeline_mode=pl.Buffered(1))


def _layer_resident(stacked, layer):
  nd = stacked.ndim
  return pl.BlockSpec((1,) + stacked.shape[1:], lambda *_: (layer,) + (0,) * (nd - 1),
                      pipeline_mode=pl.Buffered(1))


def _rope_t(x_t, cos_t, sin_t):
  x1, x2 = x_t[:ROPE_HALF], x_t[ROPE_HALF:]
  return jnp.concatenate([x1 * cos_t - x2 * sin_t, x2 * cos_t + x1 * sin_t], axis=0)


def _mla_proj_kernel(x_ref, g_ref, wdkv_ref, qn_ref, kvn_ref, wqnt_ref, wqrt_ref,
                     wuk_ref, pos_ref, freq_ref, qt_ref, kc_ref, ct_ref, wql_ref):
  rows = x_ref.shape[0]
  scale = QK_HEAD_DIM ** -0.5 * math.log2(math.e)

  @pl.when(pl.program_id(0) == 0)
  def _():
    for hd in range(MLA_HEADS):
      rws = slice(hd * QK_NOPE_DIM, (hd + 1) * QK_NOPE_DIM)
      wql_ref[rws, :] = (_dot(wuk_ref[0, hd], wqnt_ref[0, rws, :]) * scale).astype(BF16)

  tq = ATTN_Q
  tiles = [slice(r0, r0 + tq) for r0 in range(0, rows, tq)]
  pad = jnp.zeros((V7X_LANES - QK_ROPE_DIM, tq), F32)
  lats = [_dot(_rms(x_ref[tl, :], g_ref[0]).astype(BF16), wdkv_ref[0])
          for tl in tiles]
  tables = []
  for tl in tiles:
    ang = freq_ref[...] * pos_ref[:, tl]
    tables.append((jnp.cos(ang), jnp.sin(ang)))

  cq_ts = []
  for ti, (tl, lat, (cos_t, sin_t)) in enumerate(zip(tiles, lats, tables)):
    c_q = _rms(lat[:, :Q_LORA_RANK], qn_ref[0])
    c_kv = _rms(lat[:, Q_LORA_RANK:Q_LORA_RANK + KV_LORA_RANK], kvn_ref[0])
    kr_t = lat[:, Q_LORA_RANK + KV_LORA_RANK:].T
    kr_roped = jnp.concatenate([_rope_t(kr_t[:QK_ROPE_DIM], cos_t, sin_t), pad], axis=0).T
    kc_ref[tl, :] = jnp.concatenate([c_kv.astype(BF16), kr_roped.astype(BF16)], axis=1)
    ct_ref[ti] = jnp.concatenate(
        [c_kv.T, jnp.ones((V_ROWS - KV_LORA_RANK, tq), F32)], axis=0).astype(BF16)
    cq_ts.append(c_q.T.astype(BF16))

  qs = [(_dot(wql_ref[...], cq_t), _dot(wqrt_ref[0], cq_t)) for cq_t in cq_ts]
  for ti, ((ql_t, qr_t), (cos_t, sin_t)) in enumerate(zip(qs, tables)):
    for hd in range(MLA_HEADS):
      q_lat = ql_t[hd * KV_LORA_RANK:(hd + 1) * KV_LORA_RANK]
      q_rope = _rope_t(qr_t[hd * QK_ROPE_DIM:(hd + 1) * QK_ROPE_DIM], cos_t, sin_t) * scale
      qt_ref[ti, :, hd * tq:(hd + 1) * tq] = jnp.concatenate(
          [q_lat, q_rope, pad], axis=0).astype(BF16)


def _mla_proj(x, g, qn, kvn, weights, layer, pos_row, freq_col):
  wdkv, wqnt, wqrt, wuk = weights
  t, d = x.shape
  tq, per_step = ATTN_Q, PROJ_TILES_PER_STEP
  rows = per_step * tq
  steps = t // rows
  row_spec = lambda width: pl.BlockSpec((rows, width), lambda i: (i, 0))
  in_specs = [row_spec(d), _layer_resident(*g), _layer_resident(wdkv, layer),
              _layer_resident(*qn), _layer_resident(*kvn), _layer_resident(wqnt, layer),
              _layer_resident(wqrt, layer), _layer_resident(wuk, layer),
              pl.BlockSpec((1, rows), lambda i: (0, i)), _resident(freq_col.shape)]
  out_shape = [
      jax.ShapeDtypeStruct((t // tq, QK_PAD, MLA_HEADS * tq), BF16),
      jax.ShapeDtypeStruct((t, QK_PAD), BF16),
      jax.ShapeDtypeStruct((t // tq, V_ROWS, tq), BF16),
  ]
  out_specs = [
      pl.BlockSpec((per_step, QK_PAD, MLA_HEADS * tq), lambda i: (i, 0, 0)),
      row_spec(QK_PAD),
      pl.BlockSpec((per_step, V_ROWS, tq), lambda i: (i, 0, 0)),
  ]
  blocks = (sum(_nbytes(w.shape[1:], w.dtype) for w in weights)
            + 2 * (_nbytes((rows, d), F32) + _nbytes((MLA_HEADS + 2, rows, QK_PAD), BF16)))
  temps = 6 * _nbytes((rows, MLA_HEADS * QK_HEAD_DIM), F32)
  return pl.pallas_call(
      _mla_proj_kernel,
      grid=(steps,),
      in_specs=in_specs,
      out_specs=out_specs,
      out_shape=out_shape,
      scratch_shapes=[pltpu.VMEM((MLA_HEADS * KV_LORA_RANK, Q_LORA_RANK), BF16)],
      compiler_params=pltpu.CompilerParams(
          dimension_semantics=("arbitrary",),
          vmem_limit_bytes=_vmem_limit(blocks, temps)),
      name="mla_proj",
  )(x, g[0], wdkv, qn[0], kvn[0], wqnt, wqrt, wuk, pos_row, freq_col)


_FULL, _DIAG, _SKIP = "full", "diag", "skip"


def _mla_attn_kernel(qt_ref, kc_ref, ct_ref, qn_ref, kn_ref, o_ref, s0_ref, s1_ref,
                     mx0_ref, mx1_ref, m_ref, acc_ref):
  qi = pl.program_id(1)
  tk = ATTN_K
  tq = qt_ref.shape[3] // MLA_HEADS
  units = qt_ref.shape[3] // tk
  bufs = ((s0_ref, mx0_ref), (s1_ref, mx1_ref))

  m_ref[...] = jnp.full(m_ref.shape, -jnp.inf, F32)
  acc_ref[...] = jnp.zeros(acc_ref.shape, F32)

  def keys(kt):
    return kc_ref[0, pl.ds(pl.multiple_of(kt * tk, tk), tk), :]

  def score_unit(k, q_ref, buf, u):
    s_ref, mx_ref = bufs[buf]
    cols = slice(u * tk, (u + 1) * tk)
    s = _dot(k, q_ref[0, 0, :, cols])
    s_ref[:, cols] = s
    mx_ref[:, cols] = jnp.max(s, axis=0, keepdims=True)

  def softmax_unit(v_t, buf, u, mode):
    s_ref, mx_ref = bufs[buf]
    cols = slice(u * tk, (u + 1) * tk)
    s = s_ref[:, cols]
    if mode == _DIAG:
      key = lax.broadcasted_iota(jnp.int32, s.shape, 0)
      qry = lax.broadcasted_iota(jnp.int32, s.shape, 1)
      s = jnp.where(key <= qry, s, -jnp.inf)
      mx = jnp.max(s, axis=0, keepdims=True)
    else:
      mx = mx_ref[:, cols]
    m_prev = m_ref[:, cols]
    m_new = jnp.maximum(m_prev, mx)
    alpha = jnp.exp2(m_prev - m_new)
    p = jnp.exp2(s - m_new).astype(BF16)
    acc_ref[:, cols] = alpha * acc_ref[:, cols] + _dot(v_t, p)
    m_ref[:, cols] = m_new

  def step(nxt, next_units, sb, half, buf, mode_of_unit):
    v_t = None if sb is None else ct_ref[0, sb, :, half * tk:(half + 1) * tk]
    for u in range(units):
      if u in next_units:
        score_unit(nxt[0], nxt[1], 1 - buf, u)
      if sb is not None and mode_of_unit(u) != _SKIP:
        softmax_unit(v_t, buf, u, mode_of_unit(u))

  all_units = range(units)

  @pl.when(jnp.logical_and(pl.program_id(0) == 0, qi == 0))
  def _():
    step((keys(0), qt_ref), all_units, None, None, 1, None)

  def body(j, carry):
    step((keys(2 * j + 1), qt_ref), all_units, j, 0, 0, lambda u: _FULL)
    step((keys(2 * j + 2), qt_ref), all_units, j, 1, 1, lambda u: _FULL)
    return carry

  lax.fori_loop(0, qi, body, 0)
  step((keys(2 * qi + 1), qt_ref), range(1, units, 2), qi, 0, 0,
       lambda u: _DIAG if u % 2 == 0 else _FULL)
  step((kn_ref[0], qn_ref), all_units, qi, 1, 1, lambda u: _SKIP if u % 2 == 0 else _DIAG)

  o_t = acc_ref[:KV_LORA_RANK, :] / acc_ref[KV_LORA_RANK:KV_LORA_RANK + 1, :]
  for hd in range(MLA_HEADS):
    o_ref[0, :, hd * KV_LORA_RANK:(hd + 1) * KV_LORA_RANK] = (
        o_t[:, hd * tq:(hd + 1) * tq].T.astype(o_ref.dtype))


def _mla_attn(q_t, kc, ct, batch, seq):
  tq, tk = ATTN_Q, ATTN_K
  assert tq == 2 * tk
  nq = MLA_HEADS * tq
  q_tiles = seq // tq
  qt4 = q_t.reshape(batch, q_tiles, QK_PAD, nq)
  kc3 = kc.reshape(batch, seq, QK_PAD)
  ct4 = ct.reshape(batch, q_tiles, V_ROWS, tq)
  out_w = MLA_HEADS * KV_LORA_RANK
  blocks = 2 * (2 * _nbytes((QK_PAD, nq), BF16) + _nbytes((seq + tk, QK_PAD), BF16)
                + _nbytes((V_ROWS, seq), BF16) + _nbytes((tq, out_w), BF16))

  def next_tile(b, i):
    flat = jnp.minimum(b * q_tiles + i + 1, batch * q_tiles - 1)
    return flat // q_tiles, flat % q_tiles

  scratch = _nbytes((2 * tk + V_ROWS + 24, nq), F32)
  temps = 2 * _nbytes((tk, nq), F32)
  return pl.pallas_call(
      _mla_attn_kernel,
      grid=(batch, q_tiles),
      in_specs=[
          pl.BlockSpec((1, 1, QK_PAD, nq), lambda b, i: (b, i, 0, 0)),
          pl.BlockSpec((1, seq, QK_PAD), lambda b, i: (b, 0, 0)),
          pl.BlockSpec((1, q_tiles, V_ROWS, tq), lambda b, i: (b, 0, 0, 0)),
          pl.BlockSpec((1, 1, QK_PAD, nq), lambda b, i: (*next_tile(b, i), 0, 0)),
          pl.BlockSpec((1, tk, QK_PAD), lambda b, i: (next_tile(b, i)[0], 0, 0)),
      ],
      out_specs=pl.BlockSpec((1, tq, out_w), lambda b, i: (b, i, 0)),
      out_shape=jax.ShapeDtypeStruct((batch, seq, out_w), BF16),
      scratch_shapes=[pltpu.VMEM((tk, nq), F32), pltpu.VMEM((tk, nq), F32),
                      pltpu.VMEM((1, nq), F32), pltpu.VMEM((1, nq), F32),
                      pltpu.VMEM((1, nq), F32), pltpu.VMEM((V_ROWS, nq), F32)],
      compiler_params=pltpu.CompilerParams(
          dimension_semantics=("arbitrary", "arbitrary"),
          vmem_limit_bytes=_vmem_limit(blocks + scratch, temps)),
      name="mla_attn",
  )(qt4, kc3, ct4, qt4, kc3)


def _cast_chunk_spec(stacked, layer):
  _, r, c = stacked.shape
  return pl.BlockSpec((1, r // CAST_STEPS, c),
                      lambda s: (layer, jnp.minimum(s, CAST_STEPS - 1), 0))


def _cast_chunk(step, src_ref, dst_ref):
  chunk = src_ref.shape[1]
  dst_ref[pl.ds(pl.multiple_of(step * chunk, chunk), chunk), :] = src_ref[0].astype(BF16)


def _tile_spec(rows, width):
  return pl.BlockSpec((rows, width), lambda s: (jnp.maximum(s - CAST_STEPS, 0), 0))


def _sgu_kernel(x_ref, g_ref, win_ref, lng_ref, lnb_ref, wsp_ref, bsp_ref, wout_ref,
                o_ref, win_bf, wout_bf, gated_ref):
  step = pl.program_id(0)

  @pl.when(step < CAST_STEPS)
  def _():
    _cast_chunk(step, win_ref, win_bf)
    _cast_chunk(step, wout_ref, wout_bf)

  @pl.when(step >= CAST_STEPS)
  def _():
    rows = x_ref.shape[0]
    width = lng_ref.shape[2]
    group_dim = width // SGU_GROUPS
    t_idx = lax.broadcasted_iota(jnp.int32, (SGU_CHUNK, SGU_CHUNK), 0)
    s_idx = lax.broadcasted_iota(jnp.int32, (SGU_CHUNK, SGU_CHUNK), 1)
    causal = s_idx <= t_idx
    w_sp = [jnp.where(causal, wsp_ref[0, gi], 0.0).astype(BF16)
            for gi in range(SGU_GROUPS)]

    subs = [slice(r0, r0 + SGU_SUB_ROWS) for r0 in range(0, rows, SGU_SUB_ROWS)]
    zs = [_dot(_rms(x_ref[sub, :], g_ref[0]).astype(BF16), win_bf[...]) for sub in subs]
    for sub, z in zip(subs, zs):
      r0 = sub.start
      z = 0.5 * z * (1.0 + lax.erf(z * (0.5 ** 0.5)))
      u = z[:, :width]
      v = z[:, width:]
      mu = jnp.mean(v, axis=-1, keepdims=True)
      var = jnp.mean(jnp.square(v - mu), axis=-1, keepdims=True)
      v = ((v - mu) * lax.rsqrt(var + LN_EPS) * lng_ref[0] + lnb_ref[0]).astype(BF16)
      for gi in range(SGU_GROUPS):
        bias = jnp.broadcast_to(bsp_ref[0, :, gi:gi + 1], (SGU_CHUNK, group_dim))
        cols = slice(gi * group_dim, (gi + 1) * group_dim)
        for c0 in range(0, SGU_SUB_ROWS, SGU_CHUNK):
          rws = slice(c0, c0 + SGU_CHUNK)
          mixed = _dot(w_sp[gi], v[rws, cols]) + bias
          gated_ref[r0 + c0:r0 + c0 + SGU_CHUNK, cols] = (u[rws, cols] * mixed).astype(BF16)
      o_ref[sub, :] = x_ref[sub, :] + _dot(gated_ref[sub, :], wout_bf[...])


def _sgu(x, g, win, lng, lnb, wsp, bsp_t, wout, layer):
  t, d = x.shape
  rows = MLP_ROWS
  width = lng.shape[2]
  scratch = [pltpu.VMEM(win.shape[1:], BF16), pltpu.VMEM(wout.shape[1:], BF16),
             pltpu.VMEM((rows, width), BF16)]
  blocks = (2 * (_nbytes(win.shape[1:], F32) + _nbytes(wout.shape[1:], F32)) // CAST_STEPS
            + _nbytes(win.shape[1:], BF16) + _nbytes(wout.shape[1:], BF16)
            + _nbytes(wsp.shape[1:], F32)
            + 4 * _nbytes((rows, d), F32) + _nbytes((rows, width), BF16))
  temps = 3 * _nbytes((rows, 2 * width), F32)
  return pl.pallas_call(
      _sgu_kernel,
      grid=(CAST_STEPS + t // rows,),
      in_specs=[_tile_spec(rows, d), _layer_resident(*g),
                _cast_chunk_spec(win, layer), _layer_resident(lng, layer),
                _layer_resident(lnb, layer), _layer_resident(wsp, layer),
                _layer_resident(bsp_t, layer), _cast_chunk_spec(wout, layer)],
      out_specs=_tile_spec(rows, d),
      out_shape=jax.ShapeDtypeStruct((t, d), F32),
      scratch_shapes=scratch,
      compiler_params=pltpu.CompilerParams(
          dimension_semantics=("arbitrary",),
          vmem_limit_bytes=_vmem_limit(blocks, temps)),
      name="sgu_mixer",
  )(x, g[0], win, lng, lnb, wsp, bsp_t, wout)


def _ffn_kernel(*refs, mla_out, final_norm):
  refs = list(refs)
  x_ref = refs.pop(0)
  if mla_out:
    ol_ref, wuv_ref, wo_ref = refs[:3]
    del refs[:3]
  g_ref, wup_ref, wdn_ref = refs[:3]
  del refs[:3]
  gf_ref = refs.pop(0) if final_norm else None
  o_ref = refs.pop(0)
  wo_bf = refs.pop(0) if mla_out else None
  wup_bf, wdn_bf = refs
  step = pl.program_id(0)

  @pl.when(step < CAST_STEPS)
  def _():
    _cast_chunk(step, wup_ref, wup_bf)
    _cast_chunk(step, wdn_ref, wdn_bf)
    if mla_out:
      rws = pl.ds(pl.multiple_of(step * V_HEAD_DIM, V_HEAD_DIM), V_HEAD_DIM)
      wo_bf[rws, :] = _dot(wuv_ref[0, step], wo_ref[0].astype(BF16)).astype(BF16)

  @pl.when(step >= CAST_STEPS)
  def _():
    x = x_ref[...]
    if mla_out:
      x = x + _dot(ol_ref[...], wo_bf[...])
    h = _rms(x, g_ref[0]).astype(BF16)
    y = x
    hidden = wup_bf.shape[1]
    for c0 in range(0, hidden, FFN_HIDDEN_CHUNK if x.shape[0] > MLP_ROWS else hidden):
      c1 = c0 + (FFN_HIDDEN_CHUNK if x.shape[0] > MLP_ROWS else hidden)
      a = jnp.square(jnp.maximum(_dot(h, wup_bf[:, c0:c1]), 0.0)).astype(BF16)
      y = y + _dot(a, wdn_bf[c0:c1, :])
    if final_norm:
      y = _rms(y, gf_ref[0])
    o_ref[...] = y


def _ffn(x, g, wup, wdn, layer, mla_out=None, final_g=None):
  t, d = x.shape
  rows = MLP_ROWS if mla_out is not None else FFN_PLAIN_ROWS
  hidden = wup.shape[2]
  args = [x]
  in_specs = [_tile_spec(rows, d)]
  scratch = []
  blocks = 4 * _nbytes((rows, d), F32)
  chunked = [wup, wdn]
  if mla_out is not None:
    o_lat, wuv, wo, wo_layer = mla_out
    assert CAST_STEPS == MLA_HEADS and wo.shape[1] == MLA_HEADS * V_HEAD_DIM
    args += [o_lat, wuv, wo]
    in_specs += [_tile_spec(rows, o_lat.shape[1]), _layer_resident(wuv, wo_layer),
                 _cast_chunk_spec(wo, wo_layer)]
    scratch.append(pltpu.VMEM(wo.shape[1:], BF16))
    blocks += 2 * _nbytes((rows, o_lat.shape[1]), BF16) + _nbytes(wuv.shape[1:], BF16)
    chunked.append(wo)
  args += [g, wup, wdn]
  in_specs += [_layer_resident(g, layer), _cast_chunk_spec(wup, layer), _cast_chunk_spec(wdn, layer)]
  scratch += [pltpu.VMEM(wup.shape[1:], BF16), pltpu.VMEM(wdn.shape[1:], BF16)]
  blocks += sum(_nbytes(w.shape[1:], BF16) + 2 * _nbytes(w.shape[1:], F32) // CAST_STEPS
                for w in chunked)
  if final_g is not None:
    args.append(final_g)
    in_specs.append(_layer_resident(final_g, 0))
  chunk = hidden if rows <= MLP_ROWS else FFN_HIDDEN_CHUNK
  temps = _nbytes((rows, chunk), F32) + _nbytes((rows, chunk), BF16) + 3 * _nbytes((rows, d), F32)
  return pl.pallas_call(
      functools.partial(_ffn_kernel, mla_out=mla_out is not None, final_norm=final_g is not None),
      grid=(CAST_STEPS + t // rows,),
      in_specs=in_specs,
      out_specs=_tile_spec(rows, d),
      out_shape=jax.ShapeDtypeStruct((t, d), F32),
      scratch_shapes=scratch,
      compiler_params=pltpu.CompilerParams(
          dimension_semantics=("arbitrary",),
          vmem_limit_bytes=_vmem_limit(blocks, temps)),
      name="ffn",
  )(*args)


def kernel(x, positions, norm_mix, norm_ffn, final_norm, mla_w_dkv, mla_q_norm, mla_kv_norm, mla_w_uq, mla_w_ukv, mla_w_o, sgu_w_in, sgu_ln_g, sgu_ln_b, sgu_w_spatial, sgu_b_spatial, sgu_w_out, ffn_w_up, ffn_w_down):
  batch, seq, d = x.shape
  depth = norm_mix.shape[0]
  t = batch * seq
  assert seq % MLP_ROWS == 0 and seq % ATTN_Q == 0
  xf = x.reshape(t, d)

  inv_freq = ROPE_THETA ** (-jnp.arange(0, QK_ROPE_DIM, 2, dtype=F32) / QK_ROPE_DIM)
  freq_col = inv_freq.reshape(ROPE_HALF, 1)
  pos_row = positions.astype(F32).reshape(1, t)

  vec = lambda a: a.reshape(a.shape[0], 1, a.shape[1])
  g_mix, g_ffn, g_final = vec(norm_mix), vec(norm_ffn), final_norm.reshape(1, 1, d)
  q_norm, kv_norm = vec(mla_q_norm), vec(mla_kv_norm)
  ln_g, ln_b = vec(sgu_ln_g), vec(sgu_ln_b)
  b_spatial_t = sgu_b_spatial.transpose(0, 2, 1)

  n_mla = mla_w_dkv.shape[0]
  wdkv = jnp.pad(mla_w_dkv, ((0, 0), (0, 0), (0, V7X_LANES - QK_ROPE_DIM))).astype(BF16)
  wuq = mla_w_uq.reshape(n_mla, Q_LORA_RANK, MLA_HEADS, QK_HEAD_DIM)
  wqnt = (wuq[..., :QK_NOPE_DIM].reshape(n_mla, Q_LORA_RANK, -1)
          .transpose(0, 2, 1).astype(BF16))
  wqrt = (wuq[..., QK_NOPE_DIM:].reshape(n_mla, Q_LORA_RANK, -1)
          .transpose(0, 2, 1).astype(BF16))
  wukv = mla_w_ukv.reshape(n_mla, KV_LORA_RANK, MLA_HEADS, QK_NOPE_DIM + V_HEAD_DIM)
  wuk = wukv[..., :QK_NOPE_DIM].transpose(0, 2, 1, 3).astype(BF16)
  wuv = wukv[..., QK_NOPE_DIM:].transpose(0, 2, 1, 3).astype(BF16)
  for i in range(depth):
    j = i // 2
    mla_out = None
    if i % 2 == 0:
      q_t, kc, ct = _mla_proj(xf, (g_mix, i), (q_norm, j), (kv_norm, j),
                              (wdkv, wqnt, wqrt, wuk), j, pos_row, freq_col)
      o_lat = _mla_attn(q_t, kc, ct, batch, seq)
      mla_out = (o_lat.reshape(t, -1), wuv, mla_w_o, j)
    else:
      xf = _sgu(xf, (g_mix, i), sgu_w_in, ln_g, ln_b, sgu_w_spatial, b_spatial_t,
                sgu_w_out, j)
    last = i == depth - 1
    xf = _ffn(xf, g_ffn, ffn_w_up, ffn_w_down, i,
              mla_out=mla_out, final_g=g_final if last else None)
  return xf.reshape(batch, seq, d)
```
